```python
import math
import jax, jax.numpy as jnp
from jax import lax
import numpy as np

D_MODEL = 1024
BATCH = 1
SEQ = 16384
DEPTH = 4
DEC_BATCH = 32
DEC_SEQ = 2048
PAST_LEN = 128

HEAD_DIM = 64
BLOCK = 128
WINDOW = 128
ROPE_THETA = 10000.0
DA_HEADS = 4
DA_VDIM = 2 * HEAD_DIM
WG_HEADS = 8
WG_KV = 2
WG_GROUP = WG_HEADS // WG_KV
MIX_WIDTH = DA_HEADS * DA_VDIM + WG_HEADS * HEAD_DIM
DA_Q = DA_HEADS * 2 * HEAD_DIM
DA_K = DA_HEADS * 2 * HEAD_DIM
DA_V = DA_HEADS * DA_VDIM
WG_Q = WG_HEADS * HEAD_DIM
WG_K = WG_KV * HEAD_DIM
WG_V = WG_KV * HEAD_DIM
IN_WIDTH = DA_Q + DA_K + DA_V + WG_Q + WG_K + WG_V
SPLITS = [DA_Q, DA_Q + DA_K, DA_Q + DA_K + DA_V, DA_Q + DA_K + DA_V + WG_Q,
          DA_Q + DA_K + DA_V + WG_Q + WG_K]
D_FF = 2816
N_SUB = 3
NORM_EPS = 1e-6
SUBLN_EPS = 1e-5

kernel_name = "hybrid_diffattn_windowgqa_macaron_encoder"


def rms_norm(x, gain, eps=NORM_EPS):
    x32 = x.astype(jnp.float32)
    y = x32 * lax.rsqrt(jnp.mean(x32 * x32, axis=-1, keepdims=True) + eps)
    return (y * gain.astype(jnp.float32)).astype(x.dtype)


def rms_norm_plain(x, eps=SUBLN_EPS):
    x32 = x.astype(jnp.float32)
    y = x32 * lax.rsqrt(jnp.mean(x32 * x32, axis=-1, keepdims=True) + eps)
    return y.astype(x.dtype)


def rope_tables(seq_len):
    pos = jnp.arange(seq_len, dtype=jnp.float32)
    inv_freq = 1.0 / (ROPE_THETA ** (jnp.arange(0, HEAD_DIM, 2, dtype=jnp.float32) / HEAD_DIM))
    ang = pos[:, None] * inv_freq[None, :]
    ang = jnp.concatenate([ang, ang], axis=-1)
    return jnp.cos(ang), jnp.sin(ang)


def apply_rope(x, cos, sin):
    shape = (1, x.shape[1]) + (1,) * (x.ndim - 3) + (HEAD_DIM,)
    x32 = x.astype(jnp.float32)
    x1, x2 = jnp.split(x32, 2, axis=-1)
    rot = jnp.concatenate([-x2, x1], axis=-1)
    return (x32 * cos.reshape(shape) + rot * sin.reshape(shape)).astype(x.dtype)


def swiglu(h, w_gate, w_up, w_down):
    return (jax.nn.silu(h @ w_gate) * (h @ w_up)) @ w_down


def diff_attention(q, k, v, lam, lambda_init):
    b, s = q.shape[0], q.shape[1]
    nb = s // BLOCK
    scale = HEAD_DIM ** -0.5
    qb = (q * scale).reshape(b, nb, BLOCK, DA_HEADS, 2, HEAD_DIM).transpose(1, 0, 2, 3, 4, 5)

    def one_block(qi):
        logits = jnp.einsum('bqhcd,bkhcd->bhcqk', qi, k).astype(jnp.float32)
        p = jax.nn.softmax(logits, axis=-1)
        a = p[:, :, 0] - lam * p[:, :, 1]
        return jnp.einsum('bhqk,bkhe->bqhe', a.astype(v.dtype), v)

    o = lax.map(one_block, qb)
    o = o.transpose(1, 0, 2, 3, 4).reshape(b, s, DA_HEADS, DA_VDIM)
    o = rms_norm_plain(o) * (1.0 - lambda_init)
    return o.reshape(b, s, DA_HEADS * DA_VDIM)


def window_gqa(q, k, v, sink):
    b, s = q.shape[0], q.shape[1]
    nb = s // BLOCK
    scale = HEAD_DIM ** -0.5
    qb = (q * scale).reshape(b, nb, BLOCK, WG_KV, WG_GROUP, HEAD_DIM).transpose(1, 0, 2, 3, 4, 5)
    pad = ((0, 0), (BLOCK, BLOCK), (0, 0), (0, 0))
    kp = jnp.pad(k, pad)
    vp = jnp.pad(v, pad)
    offs_q = jnp.arange(BLOCK)
    offs_k = jnp.arange(3 * BLOCK) - BLOCK
    in_band = jnp.abs(offs_k[None, :] - offs_q[:, None]) <= WINDOW
    sink_l = sink.astype(jnp.float32).reshape(1, WG_KV, WG_GROUP, 1, 1)

    def one_block(args):
        i, qi = args
        start = i * BLOCK
        kb = lax.dynamic_slice_in_dim(kp, start, 3 * BLOCK, axis=1)
        vb = lax.dynamic_slice_in_dim(vp, start, 3 * BLOCK, axis=1)
        kpos = start + offs_k
        valid = in_band & ((kpos >= 0) & (kpos < s))[None, :]
        logits = jnp.einsum('bqhgd,bkhd->bhgqk', qi, kb).astype(jnp.float32)
        logits = jnp.where(valid, logits, -jnp.inf)
        sk = jnp.broadcast_to(sink_l, logits.shape[:-1] + (1,))
        p = jax.nn.softmax(jnp.concatenate([logits, sk], axis=-1), axis=-1)[..., :-1]
        o = jnp.einsum('bhgqk,bkhd->bqhgd', p.astype(vb.dtype), vb)
        return o.reshape(b, BLOCK, WG_HEADS * HEAD_DIM)

    o = lax.map(one_block, (jnp.arange(nb), qb))
    return o.transpose(1, 0, 2, 3).reshape(b, s, WG_HEADS * HEAD_DIM)


def token_mixer(h, layer_idx, w_in, w_out, lq1, lk1, lq2, lk2, sink):
    b, s, _ = h.shape
    proj = h @ w_in
    qa, ka, va, qw, kw, vw = jnp.split(proj, SPLITS, axis=-1)
    cos, sin = rope_tables(s)
    qa = apply_rope(qa.reshape(b, s, DA_HEADS, 2, HEAD_DIM), cos, sin)
    ka = apply_rope(ka.reshape(b, s, DA_HEADS, 2, HEAD_DIM), cos, sin)
    va = va.reshape(b, s, DA_HEADS, DA_VDIM)
    lambda_init = 0.8 - 0.6 * math.exp(-0.3 * layer_idx)
    f32 = jnp.float32
    lam = (jnp.exp(jnp.sum(lq1.astype(f32) * lk1.astype(f32)))
           - jnp.exp(jnp.sum(lq2.astype(f32) * lk2.astype(f32))) + lambda_init)
    oa = diff_attention(qa, ka, va, lam, lambda_init)
    qw = apply_rope(qw.reshape(b, s, WG_HEADS, HEAD_DIM), cos, sin)
    kw = apply_rope(kw.reshape(b, s, WG_KV, HEAD_DIM), cos, sin)
    vw = vw.reshape(b, s, WG_KV, HEAD_DIM)
    ow = window_gqa(qw, kw, vw, sink)
    return jnp.concatenate([oa, ow], axis=-1) @ w_out


def encoder_layer(x, c, layer_idx, w_mod, b_mod, g_pre, g_post, w_ff_gate, w_ff_up, w_ff_down,
                  w_in, w_out, lq1, lk1, lq2, lk2, sink):
    b = x.shape[0]
    mod = (jax.nn.silu(c) @ w_mod + b_mod).reshape(b, N_SUB, 3, D_MODEL)

    def pre(z, j):
        hn = rms_norm(z, g_pre[j])
        return hn * (1.0 + mod[:, j, 1][:, None, :]) + mod[:, j, 0][:, None, :]

    def post(y, j):
        return mod[:, j, 2][:, None, :] * rms_norm(y, g_post[j])

    x = x + 0.5 * post(swiglu(pre(x, 0), w_ff_gate[0], w_ff_up[0], w_ff_down[0]), 0)
    x = x + post(token_mixer(pre(x, 1), layer_idx, w_in, w_out, lq1, lk1, lq2, lk2, sink), 1)
    x = x + 0.5 * post(swiglu(pre(x, 2), w_ff_gate[1], w_ff_up[1], w_ff_down[1]), 2)
    return x


def setup_inputs(seed: int = 0) -> dict:
    key = jax.random.key(seed)
    ks = jax.random.split(key, 20)
    f32 = jnp.float32
    nrm = lambda k, shape, sc: jax.random.normal(k, shape, f32) * sc
    return {
        "x_prompt": nrm(ks[0], (BATCH, SEQ, D_MODEL), 1.0),
        "x_sample": nrm(ks[1], (DEC_BATCH, DEC_SEQ, D_MODEL), 1.0),
        "c_prompt": nrm(ks[2], (BATCH, D_MODEL), 1.0),
        "c_sample": nrm(ks[3], (DEC_BATCH, D_MODEL), 1.0),
        "w_mod": nrm(ks[4], (DEPTH, D_MODEL, N_SUB * 3 * D_MODEL), 0.5 * D_MODEL ** -0.5),
        "b_mod": nrm(ks[5], (DEPTH, N_SUB * 3 * D_MODEL), 0.02),
        "norm_pre": 1.0 + nrm(ks[6], (DEPTH, N_SUB, D_MODEL), 0.05),
        "norm_post": 1.0 + nrm(ks[7], (DEPTH, N_SUB, D_MODEL), 0.05),
        "w_ff_gate": nrm(ks[8], (DEPTH, 2, D_MODEL, D_FF), D_MODEL ** -0.5),
        "w_ff_up": nrm(ks[9], (DEPTH, 2, D_MODEL, D_FF), D_MODEL ** -0.5),
        "w_ff_down": nrm(ks[10], (DEPTH, 2, D_FF, D_MODEL), D_FF ** -0.5),
        "w_in": nrm(ks[11], (DEPTH, D_MODEL, IN_WIDTH), D_MODEL ** -0.5),
        "w_out": nrm(ks[12], (DEPTH, MIX_WIDTH, D_MODEL), MIX_WIDTH ** -0.5),
        "lambda_q1": nrm(ks[13], (DEPTH, HEAD_DIM), 0.1),
        "lambda_k1": nrm(ks[14], (DEPTH, HEAD_DIM), 0.1),
        "lambda_q2": nrm(ks[15], (DEPTH, HEAD_DIM), 0.1),
        "lambda_k2": nrm(ks[16], (DEPTH, HEAD_DIM), 0.1),
        "sink": nrm(ks[17], (DEPTH, WG_HEADS), 0.5),
    }


def reference(x_prompt, x_sample, c_prompt, c_sample, w_mod, b_mod, norm_pre, norm_post,
              w_ff_gate, w_ff_up, w_ff_down, w_in, w_out,
              lambda_q1, lambda_k1, lambda_q2, lambda_k2, sink):
    y_prompt = x_prompt
    y_sample = x_sample
    for l in range(DEPTH):
        lw = (w_mod[l], b_mod[l], norm_pre[l], norm_post[l], w_ff_gate[l], w_ff_up[l], w_ff_down[l],
              w_in[l], w_out[l], lambda_q1[l], lambda_k1[l], lambda_q2[l], lambda_k2[l], sink[l])
        y_prompt = encoder_layer(y_prompt, c_prompt, l, *lw)
        y_sample = encoder_layer(y_sample, c_sample, l, *lw)
    return (y_prompt, y_sample)
```

```python
import functools
import math

import jax
import jax.numpy as jnp
from jax import lax
from jax.experimental import pallas as pl
from jax.experimental.pallas import tpu as pltpu

F32 = jnp.float32
BF16 = jnp.bfloat16

D_MODEL = 1024
DEPTH = 4
HEAD_DIM = 64
HALF = HEAD_DIM // 2
WINDOW = 128
ROPE_THETA = 10000.0
DA_HEADS = 4
DA_VDIM = 2 * HEAD_DIM
WG_HEADS = 8
WG_KV = 2
WG_GROUP = WG_HEADS // WG_KV
DA_Q = DA_HEADS * 2 * HEAD_DIM
DA_K = DA_Q
DA_V = DA_HEADS * DA_VDIM
WG_Q = WG_HEADS * HEAD_DIM
WG_K = WG_KV * HEAD_DIM
WG_V = WG_KV * HEAD_DIM
MIX_WIDTH = DA_V + WG_Q
D_FF = 2816
N_SUB = 3
NORM_EPS = 1e-6
SUBLN_EPS = 1e-5
LOG2E = math.log2(math.e)
Q_SCALE = HEAD_DIM ** -0.5 * LOG2E

LANES = 128
SUBLANES = 8
TM = 512
TQ = 256
FF_CHUNK = 256
MOD_TN = 2304
VMEM_LIMIT = 52 * 1024 * 1024


def _cparams(sem):
    return pltpu.CompilerParams(dimension_semantics=sem, vmem_limit_bytes=VMEM_LIMIT)


def _resident(shape):
    return pl.BlockSpec(shape, lambda *_: (0, 0), pipeline_mode=pl.Buffered(1))


def _mod_kernel(c_ref, w_ref, b_ref, o_ref):
    c = c_ref[...]
    a = c * jax.nn.sigmoid(c)
    o_ref[0] = jnp.dot(a, w_ref[0], preferred_element_type=F32,
                       precision=lax.Precision.HIGHEST) + b_ref[0]


def _mod_call(c_all, w_mod, b_mod):
    nb = c_all.shape[0]
    n_out = w_mod.shape[-1]
    return pl.pallas_call(
        _mod_kernel,
        grid=(DEPTH, n_out // MOD_TN),
        in_specs=[
            pl.BlockSpec((nb, D_MODEL), lambda l, n: (0, 0)),
            pl.BlockSpec((1, D_MODEL, MOD_TN), lambda l, n: (l, 0, n)),
            pl.BlockSpec((1, 1, MOD_TN), lambda l, n: (l, 0, n)),
        ],
        out_specs=pl.BlockSpec((1, nb, MOD_TN), lambda l, n: (l, 0, n)),
        out_shape=jax.ShapeDtypeStruct((DEPTH, nb, n_out), F32),
        compiler_params=_cparams(("arbitrary", "arbitrary")),
        name="modulation",
    )(c_all, w_mod, b_mod.reshape(DEPTH, 1, n_out))


def _rms(x, eps):
    return x * lax.rsqrt(jnp.mean(x * x, axis=-1, keepdims=True) + eps)


def _pre(x, mod_ref, gpre_ref, j):
    shift = mod_ref[0, 3 * j:3 * j + 1, :]
    scale = mod_ref[0, 3 * j + 1:3 * j + 2, :]
    hn = _rms(x, NORM_EPS) * gpre_ref[j:j + 1, :]
    return hn * (1.0 + scale) + shift


def _post(y, mod_ref, gpost_ref, j):
    gate = mod_ref[0, 3 * j + 2:3 * j + 3, :]
    return gate * (_rms(y, NORM_EPS) * gpost_ref[j:j + 1, :])


def _ffn_kernel(x_ref, mod_ref, gpre_ref, gpost_ref, wg_ref, wu_ref, wd_ref, o_ref, a_ref, *, j):
    x = x_ref[0]
    h = _pre(x, mod_ref, gpre_ref, j).astype(BF16)
    for c in range(D_FF // FF_CHUNK):
        sl = slice(c * FF_CHUNK, (c + 1) * FF_CHUNK)
        g = jnp.dot(h, wg_ref[:, sl], preferred_element_type=F32)
        u = jnp.dot(h, wu_ref[:, sl], preferred_element_type=F32)
        a_ref[:, sl] = (g * jax.nn.sigmoid(g) * u).astype(BF16)
    y = jnp.dot(a_ref[...], wd_ref[...], preferred_element_type=F32)
    o_ref[0] = x + 0.5 * _post(y, mod_ref, gpost_ref, j)


def _ffn_call(x, mod, gpre, gpost, wg, wu, wd, j):
    b, s, _ = x.shape
    const = lambda bi, si: (0, 0)
    return pl.pallas_call(
        functools.partial(_ffn_kernel, j=j),
        grid=(b, s // TM),
        in_specs=[
            pl.BlockSpec((1, TM, D_MODEL), lambda bi, si: (bi, si, 0)),
            pl.BlockSpec((1, 3 * N_SUB, D_MODEL), lambda bi, si: (bi, 0, 0)),
            pl.BlockSpec((N_SUB, D_MODEL), const),
            pl.BlockSpec((N_SUB, D_MODEL), const),
            _resident((D_MODEL, D_FF)),
            _resident((D_MODEL, D_FF)),
            _resident((D_FF, D_MODEL)),
        ],
        out_specs=pl.BlockSpec((1, TM, D_MODEL), lambda bi, si: (bi, si, 0)),
        out_shape=jax.ShapeDtypeStruct(x.shape, F32),
        scratch_shapes=[pltpu.VMEM((TM, D_FF), BF16)],
        compiler_params=_cparams(("arbitrary", "arbitrary")),
        name="ffn",
    )(x, mod, gpre, gpost, wg, wu, wd)


def _rope_rows(y, cos_t, sin_t, n_groups):
    out = []
    for g in range(n_groups):
        blk = y[g * HEAD_DIM:(g + 1) * HEAD_DIM]
        swapped = jnp.concatenate([blk[HALF:], blk[:HALF]], axis=0)
        out.append((blk * cos_t + swapped * sin_t) * Q_SCALE)
    return jnp.concatenate(out, axis=0)


def _proj_kernel(x_ref, mod_ref, gpre_ref, wt_ref, wk_ref, cos_t_ref, sin_t_ref, cos_k_ref, sin_k_ref,
                 qa_ref, va_ref, qw_ref, vw_ref, ka_ref, kw_ref):
    h = _pre(x_ref[0], mod_ref, gpre_ref, 1).astype(BF16)

    def rows(lo, hi):
        return lax.dot_general(wt_ref[lo:hi, :], h, (((1,), (1,)), ((), ())),
                               preferred_element_type=F32)

    cos_t = cos_t_ref[...]
    sin_t = sin_t_ref[...]
    qa = _rope_rows(rows(0, DA_Q), cos_t, sin_t, DA_Q // HEAD_DIM).astype(BF16)
    qw = _rope_rows(rows(DA_Q + DA_V, DA_Q + DA_V + WG_Q), cos_t, sin_t, WG_Q // HEAD_DIM).astype(BF16)
    for half in range(TM // TQ):
        qa_ref[0, half] = qa[:, half * TQ:(half + 1) * TQ]
        qw_ref[0, half] = qw[:, half * TQ:(half + 1) * TQ]
    va_ref[0, 0] = rows(DA_Q, DA_Q + DA_V).astype(BF16)
    vw_ref[0, 0] = rows(DA_Q + DA_V + WG_Q, DA_Q + DA_V + WG_Q + WG_V).astype(BF16)

    k = jnp.dot(h, wk_ref[...], preferred_element_type=F32)
    cos_k = cos_k_ref[...]
    sin_k = sin_k_ref[...]
    lane = lax.broadcasted_iota(jnp.int32, (TM, LANES), 1)
    first_half = (lane % HEAD_DIM) < HALF
    for c in range((DA_K + WG_K) // LANES):
        blk = k[:, c * LANES:(c + 1) * LANES]
        swapped = jnp.where(first_half, pltpu.roll(blk, LANES - HALF, 1), pltpu.roll(blk, HALF, 1))
        r = (blk * cos_k + swapped * sin_k).astype(BF16)
        if c < DA_K // LANES:
            ka_ref[0, :, c * LANES:(c + 1) * LANES] = r
        else:
            kw_ref[0] = r


def _proj_call(x, mod, gpre, wt, wk, cos_t, sin_t, cos_k, sin_k):
    b, s, _ = x.shape
    ns = s // TM
    const = lambda bi, si: (0, 0)
    t_rows = wt.shape[0]
    return pl.pallas_call(
        _proj_kernel,
        grid=(b, ns),
        in_specs=[
            pl.BlockSpec((1, TM, D_MODEL), lambda bi, si: (bi, si, 0)),
            pl.BlockSpec((1, 3 * N_SUB, D_MODEL), lambda bi, si: (bi, 0, 0)),
            pl.BlockSpec((N_SUB, D_MODEL), const),
            pl.BlockSpec((t_rows, D_MODEL), const),
            pl.BlockSpec((D_MODEL, DA_K + WG_K), const),
            pl.BlockSpec((HEAD_DIM, TM), lambda bi, si: (0, si)),
            pl.BlockSpec((HEAD_DIM, TM), lambda bi, si: (0, si)),
            pl.BlockSpec((TM, LANES), lambda bi, si: (si, 0)),
            pl.BlockSpec((TM, LANES), lambda bi, si: (si, 0)),
        ],
        out_specs=[
            pl.BlockSpec((1, TM // TQ, DA_Q, TQ), lambda bi, si: (bi, si, 0, 0)),
            pl.BlockSpec((1, 1, DA_V, TM), lambda bi, si: (bi, si, 0, 0)),
            pl.BlockSpec((1, TM // TQ, WG_Q, TQ), lambda bi, si: (bi, si, 0, 0)),
            pl.BlockSpec((1, 1, WG_V, TM), lambda bi, si: (bi, si, 0, 0)),
            pl.BlockSpec((1, TM, DA_K), lambda bi, si: (bi, si, 0)),
            pl.BlockSpec((1, TM, WG_K), lambda bi, si: (bi, si, 0)),
        ],
        out_shape=[
            jax.ShapeDtypeStruct((b, s // TQ, DA_Q, TQ), BF16),
            jax.ShapeDtypeStruct((b, ns, DA_V, TM), BF16),
            jax.ShapeDtypeStruct((b, s // TQ, WG_Q, TQ), BF16),
            jax.ShapeDtypeStruct((b, ns, WG_V, TM), BF16),
            jax.ShapeDtypeStruct((b, s, DA_K), BF16),
            jax.ShapeDtypeStruct((b, s, WG_K), BF16),
        ],
        compiler_params=_cparams(("arbitrary", "arbitrary")),
        name="in_proj_rope",
    )(x, mod, gpre, wt, wk, cos_t, sin_t, cos_k, sin_k)


def _diff_kernel(q_ref, k_ref, v_ref, lq1_ref, lk1_ref, lq2_ref, lk2_ref, o_ref, *,
                 lambda_init, n_chunks, n_qblocks, unroll):
    lam = (jnp.exp(jnp.sum(lq1_ref[...] * lk1_ref[...], axis=-1, keepdims=True))
           - jnp.exp(jnp.sum(lq2_ref[...] * lk2_ref[...], axis=-1, keepdims=True)) + lambda_init)
    row = lax.broadcasted_iota(jnp.int32, (2 * HEAD_DIM, TQ), 0)

    def q_block(qb, carry):
        q = q_ref[0, qb]
        zero = jnp.zeros_like(q)
        qpad = jnp.concatenate([jnp.where(row < HEAD_DIM, q, zero),
                                jnp.where(row >= HEAD_DIM, q, zero)], axis=1)

        def chunk(j, st):
            m, l, acc = st
            k = k_ref[0, pl.ds(pl.multiple_of(j * TM, TM), TM), :]
            s = jnp.dot(k, qpad, preferred_element_type=F32)
            m_new = jnp.maximum(m, jnp.max(s, axis=0, keepdims=True))
            alpha = jnp.exp2(m - m_new)
            p = jnp.exp2(s - m_new)
            l = alpha * l + jnp.sum(p, axis=0, keepdims=True)
            pb = p.astype(BF16)
            vt = v_ref[0, j]
            pv = jnp.concatenate(
                [jnp.dot(vt, pb[:, :TQ], preferred_element_type=F32),
                 jnp.dot(vt, pb[:, TQ:], preferred_element_type=F32)], axis=1)
            return m_new, l, acc * alpha + pv

        init = (jnp.full((1, 2 * TQ), -jnp.inf, F32), jnp.zeros((1, 2 * TQ), F32),
                jnp.zeros((DA_VDIM, 2 * TQ), F32))
        _, l, acc = lax.fori_loop(0, n_chunks, chunk, init, unroll=unroll)
        o = acc / l
        o = o[:, :TQ] - lam * o[:, TQ:]
        o = o * lax.rsqrt(jnp.mean(o * o, axis=0, keepdims=True) + SUBLN_EPS) * (1.0 - lambda_init)
        o_ref[0, pl.ds(pl.multiple_of(qb * TQ, TQ), TQ), :] = o.T.astype(BF16)
        return carry

    lax.fori_loop(0, n_qblocks, q_block, 0)


def _diff_call(qa, ka, va, lq1, lk1, lq2, lk2, lambda_init):
    b, s, _ = ka.shape
    tq_outer = min(s, 2048)
    n_qblocks = tq_outer // TQ
    n_chunks = s // TM
    vec = pl.BlockSpec((1, HEAD_DIM), lambda bi, h, qi: (0, 0))
    return pl.pallas_call(
        functools.partial(_diff_kernel, lambda_init=lambda_init, n_chunks=n_chunks,
                          n_qblocks=n_qblocks, unroll=min(n_chunks, 4)),
        grid=(b, DA_HEADS, s // tq_outer),
        in_specs=[
            pl.BlockSpec((1, n_qblocks, 2 * HEAD_DIM, TQ), lambda bi, h, qi: (bi, qi, h, 0)),
            pl.BlockSpec((1, s, 2 * HEAD_DIM), lambda bi, h, qi: (bi, 0, h)),
            pl.BlockSpec((1, n_chunks, DA_VDIM, TM), lambda bi, h, qi: (bi, 0, h, 0)),
            vec, vec, vec, vec,
        ],
        out_specs=pl.BlockSpec((1, tq_outer, DA_VDIM), lambda bi, h, qi: (bi, qi, h)),
        out_shape=jax.ShapeDtypeStruct((b, s, DA_V), BF16),
        compiler_params=_cparams(("arbitrary", "arbitrary", "arbitrary")),
        name="diff_attention",
    )(qa, ka, va, lq1, lk1, lq2, lk2)


def _window_kernel(q_ref, kp_ref, kc_ref, kn_ref, vp_ref, vc_ref, vn_ref, sink_ref, o_ref, *, seq_len):
    c = pl.program_id(1)
    keys = jnp.concatenate([kp_ref[0], kc_ref[0], kn_ref[0]], axis=0)
    vals = jnp.concatenate([vp_ref[0, 0], vc_ref[0, 0], vn_ref[0, 0]], axis=1)
    sink = sink_ref[...] * LOG2E
    n_win = 3 * WINDOW
    kidx = lax.broadcasted_iota(jnp.int32, (n_win, WINDOW), 0)
    qidx = lax.broadcasted_iota(jnp.int32, (n_win, WINDOW), 1)
    band = jnp.abs(kidx - WINDOW - qidx) <= WINDOW
    zero_q = jnp.zeros((HEAD_DIM, WINDOW), BF16)
    for qs in range(TM // WINDOW):
        kpos = c * TM + (qs - 1) * WINDOW + kidx
        valid1 = band & (kpos >= 0) & (kpos < seq_len)
        bias1 = jnp.where(valid1, 0.0, -jnp.inf).astype(F32)
        bias = jnp.concatenate([bias1] * WG_GROUP, axis=1)
        kwin = keys[qs * WINDOW:qs * WINDOW + n_win]
        vwin = vals[:, qs * WINDOW:qs * WINDOW + n_win]
        outs = []
        for g in range(WG_KV):
            blocks = []
            for hh in range(WG_GROUP):
                hd = g * WG_GROUP + hh
                qh = q_ref[0, qs // 2, hd * HEAD_DIM:(hd + 1) * HEAD_DIM,
                           (qs % 2) * WINDOW:(qs % 2 + 1) * WINDOW]
                pad = [qh, zero_q] if g == 0 else [zero_q, qh]
                blocks.append(jnp.concatenate(pad, axis=0))
            qpad = jnp.concatenate(blocks, axis=1)
            s = jnp.dot(kwin, qpad, preferred_element_type=F32)
            s = s + bias
            sk = sink[:, g * WG_GROUP * WINDOW:(g + 1) * WG_GROUP * WINDOW]
            m = jnp.maximum(jnp.max(s, axis=0, keepdims=True), sk)
            p = jnp.exp2(s - m)
            l = jnp.sum(p, axis=0, keepdims=True) + jnp.exp2(sk - m)
            pv = jnp.dot(vwin, p.astype(BF16), preferred_element_type=F32)
            o = pv[g * HEAD_DIM:(g + 1) * HEAD_DIM] / l
            for hh in range(WG_GROUP):
                outs.append(o[:, hh * WINDOW:(hh + 1) * WINDOW])
        o_all = jnp.concatenate(outs, axis=0)
        o_ref[0, qs * WINDOW:(qs + 1) * WINDOW, :] = o_all.T.astype(BF16)


def _window_call(qw, kw, vw, sink_row):
    b, s, _ = kw.shape
    ns = s // TM
    nblk = s // WINDOW
    per = TM // WINDOW
    return pl.pallas_call(
        functools.partial(_window_kernel, seq_len=s),
        grid=(b, ns),
        in_specs=[
            pl.BlockSpec((1, TM // TQ, WG_Q, TQ), lambda bi, c: (bi, c, 0, 0)),
            pl.BlockSpec((1, WINDOW, WG_K), lambda bi, c: (bi, jnp.maximum(c * per - 1, 0), 0)),
            pl.BlockSpec((1, TM, WG_K), lambda bi, c: (bi, c, 0)),
            pl.BlockSpec((1, WINDOW, WG_K), lambda bi, c: (bi, jnp.minimum(c * per + per, nblk - 1), 0)),
            pl.BlockSpec((1, 1, WG_V, WINDOW), lambda bi, c: (bi, jnp.maximum(c - 1, 0), 0, per - 1)),
            pl.BlockSpec((1, 1, WG_V, TM), lambda bi, c: (bi, c, 0, 0)),
            pl.BlockSpec((1, 1, WG_V, WINDOW), lambda bi, c: (bi, jnp.minimum(c + 1, ns - 1), 0, 0)),
            pl.BlockSpec((1, WG_HEADS * WINDOW), lambda bi, c: (0, 0)),
        ],
        out_specs=pl.BlockSpec((1, TM, WG_Q), lambda bi, c: (bi, c, 0)),
        out_shape=jax.ShapeDtypeStruct((b, s, WG_Q), BF16),
        compiler_params=_cparams(("arbitrary", "arbitrary")),
        name="window_gqa",
    )(qw, kw, kw, kw, vw, vw, vw, sink_row)


def _out_kernel(x_ref, oa_ref, ow_ref, mod_ref, gpost_ref, wa_ref, ww_ref, o_ref):
    y = (jnp.dot(oa_ref[0], wa_ref[...], preferred_element_type=F32)
         + jnp.dot(ow_ref[0], ww_ref[...], preferred_element_type=F32))
    o_ref[0] = x_ref[0] + _post(y, mod_ref, gpost_ref, 1)


def _out_call(x, oa, ow, mod, gpost, wa, ww):
    b, s, _ = x.shape
    const = lambda bi, si: (0, 0)
    return pl.pallas_call(
        _out_kernel,
        grid=(b, s // TM),
        in_specs=[
            pl.BlockSpec((1, TM, D_MODEL), lambda bi, si: (bi, si, 0)),
            pl.BlockSpec((1, TM, DA_V), lambda bi, si: (bi, si, 0)),
            pl.BlockSpec((1, TM, WG_Q), lambda bi, si: (bi, si, 0)),
            pl.BlockSpec((1, 3 * N_SUB, D_MODEL), lambda bi, si: (bi, 0, 0)),
            pl.BlockSpec((N_SUB, D_MODEL), const),
            pl.BlockSpec((DA_V, D_MODEL), const),
            pl.BlockSpec((WG_Q, D_MODEL), const),
        ],
        out_specs=pl.BlockSpec((1, TM, D_MODEL), lambda bi, si: (bi, si, 0)),
        out_shape=jax.ShapeDtypeStruct(x.shape, F32),
        compiler_params=_cparams(("arbitrary", "arbitrary")),
        name="out_proj",
    )(x, oa, ow, mod, gpost, wa, ww)


def _rope_tables(seq_len):
    pos = jnp.arange(seq_len, dtype=F32)
    inv_freq = 1.0 / (ROPE_THETA ** (jnp.arange(0, HEAD_DIM, 2, dtype=F32) / HEAD_DIM))
    ang = pos[:, None] * inv_freq[None, :]
    cos = jnp.cos(ang)
    sin = jnp.sin(ang)
    cos_h = jnp.concatenate([cos, cos], axis=-1)
    sin_h = jnp.concatenate([-sin, sin], axis=-1)
    cos_k = jnp.concatenate([cos_h, cos_h], axis=-1)
    sin_k = jnp.concatenate([sin_h, sin_h], axis=-1)
    return cos_h.T, sin_h.T, cos_k, sin_k


def kernel(x_prompt, x_sample, c_prompt, c_sample, w_mod, b_mod, norm_pre, norm_post, w_ff_gate, w_ff_up,
           w_ff_down, w_in, w_out, lambda_q1, lambda_k1, lambda_q2, lambda_k2, sink):
    n_prompt = x_prompt.shape[0]
    c_all = jnp.concatenate([c_prompt, c_sample], axis=0)
    n_seq = c_all.shape[0]
    c_all = jnp.pad(c_all, ((0, -n_seq % SUBLANES), (0, 0)))
    mod_all = _mod_call(c_all, w_mod, b_mod)[:, :n_seq]
    mod_all = mod_all.reshape(DEPTH, n_seq, 3 * N_SUB, D_MODEL)

    wg = w_ff_gate.astype(BF16)
    wu = w_ff_up.astype(BF16)
    wd = w_ff_down.astype(BF16)
    o_qa, o_ka, o_va, o_qw, o_kw, o_vw = 0, DA_Q, DA_Q + DA_K, DA_Q + DA_K + DA_V, \
        DA_Q + DA_K + DA_V + WG_Q, DA_Q + DA_K + DA_V + WG_Q + WG_K
    w_in_b = w_in.astype(BF16)
    wt = jnp.concatenate([w_in_b[:, :, o_qa:o_ka], w_in_b[:, :, o_va:o_qw],
                          w_in_b[:, :, o_qw:o_kw], w_in_b[:, :, o_vw:]], axis=-1).transpose(0, 2, 1)
    wk = jnp.concatenate([w_in_b[:, :, o_ka:o_va], w_in_b[:, :, o_kw:o_vw]], axis=-1)
    w_out_b = w_out.astype(BF16)
    sink_rows = jnp.repeat(sink, WINDOW, axis=-1)

    groups = [(x_prompt, slice(0, n_prompt)), (x_sample, slice(n_prompt, None))]
    tables = {x.shape[1]: _rope_tables(x.shape[1]) for x, _ in groups}
    outs = []
    for x, rows in groups:
        tab = tables[x.shape[1]]
        for l in range(DEPTH):
            mod = mod_all[l, rows]
            lambda_init = 0.8 - 0.6 * math.exp(-0.3 * l)
            x = _ffn_call(x, mod, norm_pre[l], norm_post[l], wg[l, 0], wu[l, 0], wd[l, 0], 0)
            qa, va, qw, vw, ka, kw = _proj_call(x, mod, norm_pre[l], wt[l], wk[l], *tab)
            oa = _diff_call(qa, ka, va, lambda_q1[l:l + 1], lambda_k1[l:l + 1], lambda_q2[l:l + 1],
                            lambda_k2[l:l + 1], lambda_init)
            ow = _window_call(qw, kw, vw, sink_rows[l:l + 1])
            x = _out_call(x, oa, ow, mod, norm_post[l], w_out_b[l, :DA_V], w_out_b[l, DA_V:])
            x = _ffn_call(x, mod, norm_pre[l], norm_post[l], wg[l, 1], wu[l, 1], wd[l, 1], 2)
        outs.append(x)
    return tuple(outs)
```

```python
import functools
import math

import jax
import jax.numpy as jnp
from jax import lax
from jax.experimental import pallas as pl
from jax.experimental.pallas import tpu as pltpu

F32 = jnp.float32
BF16 = jnp.bfloat16

D_MODEL = 1024
DEPTH = 4
HEAD_DIM = 64
HALF = HEAD_DIM // 2
WINDOW = 128
ROPE_THETA = 10000.0
DA_HEADS = 4
DA_VDIM = 2 * HEAD_DIM
WG_HEADS = 8
WG_KV = 2
WG_GROUP = WG_HEADS // WG_KV
DA_Q = DA_HEADS * 2 * HEAD_DIM
DA_K = DA_Q
DA_V = DA_HEADS * DA_VDIM
WG_Q = WG_HEADS * HEAD_DIM
WG_K = WG_KV * HEAD_DIM
WG_V = WG_KV * HEAD_DIM
MIX_WIDTH = DA_V + WG_Q
D_FF = 2816
N_SUB = 3
NORM_EPS = 1e-6
SUBLN_EPS = 1e-5
LOG2E = math.log2(math.e)
Q_SCALE = HEAD_DIM ** -0.5 * LOG2E
BOUND_SLACK = 1.0 + 2.0 ** -12
MIN_COLUMN_SUM = 2.0 ** -80

LANES = 128
SUBLANES = 8
TM = 512
TQ = 256
FF_CHUNK = 256
MOD_TN = 2304
VMEM_LIMIT = 52 * 1024 * 1024


def _cparams(sem):
    return pltpu.CompilerParams(dimension_semantics=sem, vmem_limit_bytes=VMEM_LIMIT)


def _resident(shape):
    return pl.BlockSpec(shape, lambda *_: (0, 0), pipeline_mode=pl.Buffered(1))


def _mod_kernel(c_ref, w_ref, b_ref, o_ref):
    c = c_ref[...]
    a = c * jax.nn.sigmoid(c)
    o_ref[0] = jnp.dot(a, w_ref[0], preferred_element_type=F32,
                       precision=lax.Precision.HIGHEST) + b_ref[0]


def _mod_call(c_all, w_mod, b_mod):
    nb = c_all.shape[0]
    n_out = w_mod.shape[-1]
    return pl.pallas_call(
        _mod_kernel,
        grid=(DEPTH, n_out // MOD_TN),
        in_specs=[
            pl.BlockSpec((nb, D_MODEL), lambda l, n: (0, 0)),
            pl.BlockSpec((1, D_MODEL, MOD_TN), lambda l, n: (l, 0, n)),
            pl.BlockSpec((1, 1, MOD_TN), lambda l, n: (l, 0, n)),
        ],
        out_specs=pl.BlockSpec((1, nb, MOD_TN), lambda l, n: (l, 0, n)),
        out_shape=jax.ShapeDtypeStruct((DEPTH, nb, n_out), F32),
        compiler_params=_cparams(("arbitrary", "arbitrary")),
        name="modulation",
    )(c_all, w_mod, b_mod.reshape(DEPTH, 1, n_out))


def _rms(x, eps):
    return x * lax.rsqrt(jnp.mean(x * x, axis=-1, keepdims=True) + eps)


def _pre(x, mod_ref, gpre_ref, j):
    shift = mod_ref[0, 3 * j:3 * j + 1, :]
    scale = mod_ref[0, 3 * j + 1:3 * j + 2, :]
    hn = _rms(x, NORM_EPS) * gpre_ref[j:j + 1, :]
    return hn * (1.0 + scale) + shift


def _post(y, mod_ref, gpost_ref, j):
    gate = mod_ref[0, 3 * j + 2:3 * j + 3, :]
    return gate * (_rms(y, NORM_EPS) * gpost_ref[j:j + 1, :])


def _ffn_kernel(x_ref, mod_ref, gpre_ref, gpost_ref, wg_ref, wu_ref, wd_ref, o_ref, a_ref, *, j):
    x = x_ref[0]
    h = _pre(x, mod_ref, gpre_ref, j).astype(BF16)
    for c in range(D_FF // FF_CHUNK):
        sl = slice(c * FF_CHUNK, (c + 1) * FF_CHUNK)
        g = jnp.dot(h, wg_ref[:, sl], preferred_element_type=F32)
        u = jnp.dot(h, wu_ref[:, sl], preferred_element_type=F32)
        a_ref[:, sl] = (g * jax.nn.sigmoid(g) * u).astype(BF16)
    y = jnp.dot(a_ref[...], wd_ref[...], preferred_element_type=F32)
    o_ref[0] = x + 0.5 * _post(y, mod_ref, gpost_ref, j)


def _ffn_call(x, mod, gpre, gpost, wg, wu, wd, j):
    b, s, _ = x.shape
    const = lambda bi, si: (0, 0)
    return pl.pallas_call(
        functools.partial(_ffn_kernel, j=j),
        grid=(b, s // TM),
        in_specs=[
            pl.BlockSpec((1, TM, D_MODEL), lambda bi, si: (bi, si, 0)),
            pl.BlockSpec((1, 3 * N_SUB, D_MODEL), lambda bi, si: (bi, 0, 0)),
            pl.BlockSpec((N_SUB, D_MODEL), const),
            pl.BlockSpec((N_SUB, D_MODEL), const),
            _resident((D_MODEL, D_FF)),
            _resident((D_MODEL, D_FF)),
            _resident((D_FF, D_MODEL)),
        ],
        out_specs=pl.BlockSpec((1, TM, D_MODEL), lambda bi, si: (bi, si, 0)),
        out_shape=jax.ShapeDtypeStruct(x.shape, F32),
        scratch_shapes=[pltpu.VMEM((TM, D_FF), BF16)],
        compiler_params=_cparams(("arbitrary", "arbitrary")),
        name="ffn",
    )(x, mod, gpre, gpost, wg, wu, wd)


def _rope_rows(y, cos_t, sin_t, n_groups):
    out = []
    for g in range(n_groups):
        blk = y[g * HEAD_DIM:(g + 1) * HEAD_DIM]
        swapped = jnp.concatenate([blk[HALF:], blk[:HALF]], axis=0)
        out.append((blk * cos_t + swapped * sin_t) * Q_SCALE)
    return jnp.concatenate(out, axis=0)


def _proj_kernel(x_ref, mod_ref, gpre_ref, wt_ref, wk_ref, cos_t_ref, sin_t_ref, cos_k_ref, sin_k_ref,
                 qa_ref, va_ref, qw_ref, vw_ref, ka_ref, kw_ref):
    h = _pre(x_ref[0], mod_ref, gpre_ref, 1).astype(BF16)

    def rows(lo, hi):
        return lax.dot_general(wt_ref[lo:hi, :], h, (((1,), (1,)), ((), ())),
                               preferred_element_type=F32)

    cos_t = cos_t_ref[...]
    sin_t = sin_t_ref[...]
    qa = _rope_rows(rows(0, DA_Q), cos_t, sin_t, DA_Q // HEAD_DIM).astype(BF16)
    qw = _rope_rows(rows(DA_Q + DA_V, DA_Q + DA_V + WG_Q), cos_t, sin_t, WG_Q // HEAD_DIM).astype(BF16)
    for half in range(TM // TQ):
        qa_ref[0, half] = qa[:, half * TQ:(half + 1) * TQ]
        qw_ref[0, half] = qw[:, half * TQ:(half + 1) * TQ]
    va_ref[0, 0] = rows(DA_Q, DA_Q + DA_V).astype(BF16)
    vw_ref[0, 0] = rows(DA_Q + DA_V + WG_Q, DA_Q + DA_V + WG_Q + WG_V).astype(BF16)

    k = jnp.dot(h, wk_ref[...], preferred_element_type=F32)
    cos_k = cos_k_ref[...]
    sin_k = sin_k_ref[...]
    lane = lax.broadcasted_iota(jnp.int32, (TM, LANES), 1)
    first_half = (lane % HEAD_DIM) < HALF
    for c in range((DA_K + WG_K) // LANES):
        blk = k[:, c * LANES:(c + 1) * LANES]
        swapped = jnp.where(first_half, pltpu.roll(blk, LANES - HALF, 1), pltpu.roll(blk, HALF, 1))
        r = (blk * cos_k + swapped * sin_k).astype(BF16)
        if c < DA_K // LANES:
            ka_ref[0, :, c * LANES:(c + 1) * LANES] = r
        else:
            kw_ref[0] = r


def _proj_call(x, mod, gpre, wt, wk, cos_t, sin_t, cos_k, sin_k):
    b, s, _ = x.shape
    ns = s // TM
    const = lambda bi, si: (0, 0)
    t_rows = wt.shape[0]
    return pl.pallas_call(
        _proj_kernel,
        grid=(b, ns),
        in_specs=[
            pl.BlockSpec((1, TM, D_MODEL), lambda bi, si: (bi, si, 0)),
            pl.BlockSpec((1, 3 * N_SUB, D_MODEL), lambda bi, si: (bi, 0, 0)),
            pl.BlockSpec((N_SUB, D_MODEL), const),
            pl.BlockSpec((t_rows, D_MODEL), const),
            pl.BlockSpec((D_MODEL, DA_K + WG_K), const),
            pl.BlockSpec((HEAD_DIM, TM), lambda bi, si: (0, si)),
            pl.BlockSpec((HEAD_DIM, TM), lambda bi, si: (0, si)),
            pl.BlockSpec((TM, LANES), lambda bi, si: (si, 0)),
            pl.BlockSpec((TM, LANES), lambda bi, si: (si, 0)),
        ],
        out_specs=[
            pl.BlockSpec((1, TM // TQ, DA_Q, TQ), lambda bi, si: (bi, si, 0, 0)),
            pl.BlockSpec((1, 1, DA_V, TM), lambda bi, si: (bi, si, 0, 0)),
            pl.BlockSpec((1, TM // TQ, WG_Q, TQ), lambda bi, si: (bi, si, 0, 0)),
            pl.BlockSpec((1, 1, WG_V, TM), lambda bi, si: (bi, si, 0, 0)),
            pl.BlockSpec((1, TM, DA_K), lambda bi, si: (bi, si, 0)),
            pl.BlockSpec((1, TM, WG_K), lambda bi, si: (bi, si, 0)),
        ],
        out_shape=[
            jax.ShapeDtypeStruct((b, s // TQ, DA_Q, TQ), BF16),
            jax.ShapeDtypeStruct((b, ns, DA_V, TM), BF16),
            jax.ShapeDtypeStruct((b, s // TQ, WG_Q, TQ), BF16),
            jax.ShapeDtypeStruct((b, ns, WG_V, TM), BF16),
            jax.ShapeDtypeStruct((b, s, DA_K), BF16),
            jax.ShapeDtypeStruct((b, s, WG_K), BF16),
        ],
        compiler_params=_cparams(("arbitrary", "arbitrary")),
        name="in_proj_rope",
    )(x, mod, gpre, wt, wk, cos_t, sin_t, cos_k, sin_k)


def _diff_kernel(q_ref, k_ref, v_ref, lq1_ref, lk1_ref, lq2_ref, lk2_ref, o_ref, knorm_ref, s_ref, *,
                 lambda_init, n_chunks, n_qblocks, unroll_c, unroll_q):
    lam = (jnp.exp(jnp.sum(lq1_ref[...] * lk1_ref[...], axis=-1, keepdims=True))
           - jnp.exp(jnp.sum(lq2_ref[...] * lk2_ref[...], axis=-1, keepdims=True)) + lambda_init)
    row = lax.broadcasted_iota(jnp.int32, (2 * HEAD_DIM, TQ), 0)

    @pl.when(pl.program_id(2) == 0)
    def _():
        def body(j, mx):
            kc = k_ref[0, pl.ds(pl.multiple_of(j * TM, TM), TM), :].astype(F32)
            return jnp.maximum(mx, jnp.sum(kc * kc, axis=-1, keepdims=True))
        mx = lax.fori_loop(0, n_chunks, body, jnp.zeros((TM, 1), F32))
        knorm_ref[...] = jnp.broadcast_to(jnp.sqrt(jnp.max(mx, axis=0, keepdims=True)), knorm_ref.shape)

    def padded_q(qb):
        q = q_ref[0, qb]
        zero = jnp.zeros_like(q)
        return jnp.concatenate([jnp.where(row < HEAD_DIM, q, zero),
                                jnp.where(row >= HEAD_DIM, q, zero)], axis=1)

    def p_times_v(j, pb):
        vt = v_ref[0, j]
        return jnp.concatenate(
            [jnp.dot(vt, pb[:, :TQ], preferred_element_type=F32),
             jnp.dot(vt, pb[:, TQ:], preferred_element_type=F32)], axis=1)

    def finish(qb, l, acc):
        o = acc / l
        o = o[:, :TQ] - lam * o[:, TQ:]
        o = o * lax.rsqrt(jnp.mean(o * o, axis=0, keepdims=True) + SUBLN_EPS) * (1.0 - lambda_init)
        o_ref[0, pl.ds(pl.multiple_of(qb * TQ, TQ), TQ), :] = o.T.astype(BF16)

    def key_chunk(j):
        return k_ref[0, pl.ds(pl.multiple_of(j * TM, TM), TM), :]

    kmax = knorm_ref[0:1, :]
    kmax = jnp.concatenate([kmax] * (2 * TQ // LANES), axis=1)

    def q_block_bounded(qb, l_min):
        qpad = padded_q(qb)
        qf = qpad.astype(F32)
        shift = jnp.sqrt(jnp.sum(qf * qf, axis=0, keepdims=True)) * kmax * BOUND_SLACK

        def logits(j):
            return jnp.dot(key_chunk(j), qpad, preferred_element_type=F32)

        def chunk(j, slot, st, next_j):
            l8, acc = st
            if next_j is not None:
                s_ref[1 - slot] = logits(next_j)
            p = jnp.exp2(s_ref[slot] - shift)
            l8 = l8 + jnp.sum(p.reshape(TM // SUBLANES, SUBLANES, 2 * TQ), axis=0)
            return l8, acc + p_times_v(j, p.astype(BF16))

        st = (jnp.zeros((SUBLANES, 2 * TQ), F32), jnp.zeros((DA_VDIM, 2 * TQ), F32))
        s_ref[0] = logits(0)
        if n_chunks <= unroll_c:
            for j in range(n_chunks):
                st = chunk(j, j % 2, st, j + 1 if j + 1 < n_chunks else None)
        else:
            def group(g, st):
                for i in range(unroll_c):
                    j = g * unroll_c + i
                    st = chunk(j, i % 2, st, jnp.minimum(j + 1, n_chunks - 1))
                return st
            st = lax.fori_loop(0, n_chunks // unroll_c, group, st)
        l8, acc = st
        l = jnp.sum(l8, axis=0, keepdims=True)
        finish(qb, l, acc)
        return jnp.minimum(l_min, l)

    def q_block_online(qb, carry):
        qpad = padded_q(qb)

        def chunk(j, st):
            m, l, acc = st
            s = jnp.dot(key_chunk(j), qpad, preferred_element_type=F32)
            m_new = jnp.maximum(m, jnp.max(s, axis=0, keepdims=True))
            alpha = jnp.exp2(m - m_new)
            p = jnp.exp2(s - m_new)
            l = alpha * l + jnp.sum(p, axis=0, keepdims=True)
            return m_new, l, acc * alpha + p_times_v(j, p.astype(BF16))

        init = (jnp.full((1, 2 * TQ), -jnp.inf, F32), jnp.zeros((1, 2 * TQ), F32),
                jnp.zeros((DA_VDIM, 2 * TQ), F32))
        _, l, acc = lax.fori_loop(0, n_chunks, chunk, init)
        finish(qb, l, acc)
        return carry

    l_min = lax.fori_loop(0, n_qblocks, q_block_bounded, jnp.full((1, 2 * TQ), jnp.inf, F32),
                          unroll=unroll_q)
    bounded_ok = jnp.min(l_min) >= MIN_COLUMN_SUM

    @pl.when(jnp.logical_not(bounded_ok))
    def _():
        lax.fori_loop(0, n_qblocks, q_block_online, 0)


def _diff_call(qa, ka, va, lq1, lk1, lq2, lk2, lambda_init):
    b, s, _ = ka.shape
    tq_outer = min(s, 2048)
    n_qblocks = tq_outer // TQ
    n_chunks = s // TM
    vec = pl.BlockSpec((1, HEAD_DIM), lambda bi, h, qi: (0, 0))
    return pl.pallas_call(
        functools.partial(_diff_kernel, lambda_init=lambda_init, n_chunks=n_chunks, n_qblocks=n_qblocks,
                          unroll_c=min(n_chunks, 4), unroll_q=2 if n_chunks <= 4 else 1),
        grid=(b, DA_HEADS, s // tq_outer),
        in_specs=[
            pl.BlockSpec((1, n_qblocks, 2 * HEAD_DIM, TQ), lambda bi, h, qi: (bi, qi, h, 0)),
            pl.BlockSpec((1, s, 2 * HEAD_DIM), lambda bi, h, qi: (bi, 0, h)),
            pl.BlockSpec((1, n_chunks, DA_VDIM, TM), lambda bi, h, qi: (bi, 0, h, 0)),
            vec, vec, vec, vec,
        ],
        out_specs=pl.BlockSpec((1, tq_outer, DA_VDIM), lambda bi, h, qi: (bi, qi, h)),
        out_shape=jax.ShapeDtypeStruct((b, s, DA_V), BF16),
        scratch_shapes=[pltpu.VMEM((SUBLANES, LANES), F32), pltpu.VMEM((2, TM, 2 * TQ), F32)],
        compiler_params=_cparams(("arbitrary", "arbitrary", "arbitrary")),
        name="diff_attention",
    )(qa, ka, va, lq1, lk1, lq2, lk2)


def _window_kernel(q_ref, kp_ref, kc_ref, kn_ref, vp_ref, vc_ref, vn_ref, sink_ref, o_ref, *, seq_len):
    c = pl.program_id(1)
    keys = jnp.concatenate([kp_ref[0], kc_ref[0], kn_ref[0]], axis=0)
    vals = jnp.concatenate([vp_ref[0, 0], vc_ref[0, 0], vn_ref[0, 0]], axis=1)
    sink = sink_ref[...] * LOG2E
    n_win = 3 * WINDOW
    kidx = lax.broadcasted_iota(jnp.int32, (n_win, WINDOW), 0)
    qidx = lax.broadcasted_iota(jnp.int32, (n_win, WINDOW), 1)
    band = jnp.abs(kidx - WINDOW - qidx) <= WINDOW
    zero_q = jnp.zeros((HEAD_DIM, WINDOW), BF16)
    for qs in range(TM // WINDOW):
        kpos = c * TM + (qs - 1) * WINDOW + kidx
        valid1 = band & (kpos >= 0) & (kpos < seq_len)
        bias1 = jnp.where(valid1, 0.0, -jnp.inf).astype(F32)
        bias = jnp.concatenate([bias1] * WG_GROUP, axis=1)
        kwin = keys[qs * WINDOW:qs * WINDOW + n_win]
        vwin = vals[:, qs * WINDOW:qs * WINDOW + n_win]
        outs = []
        for g in range(WG_KV):
            blocks = []
            for hh in range(WG_GROUP):
                hd = g * WG_GROUP + hh
                qh = q_ref[0, qs // 2, hd * HEAD_DIM:(hd + 1) * HEAD_DIM,
                           (qs % 2) * WINDOW:(qs % 2 + 1) * WINDOW]
                pad = [qh, zero_q] if g == 0 else [zero_q, qh]
                blocks.append(jnp.concatenate(pad, axis=0))
            qpad = jnp.concatenate(blocks, axis=1)
            s = jnp.dot(kwin, qpad, preferred_element_type=F32)
            s = s + bias
            sk = sink[:, g * WG_GROUP * WINDOW:(g + 1) * WG_GROUP * WINDOW]
            m = jnp.maximum(jnp.max(s, axis=0, keepdims=True), sk)
            p = jnp.exp2(s - m)
            l = jnp.sum(p, axis=0, keepdims=True) + jnp.exp2(sk - m)
            pv = jnp.dot(vwin, p.astype(BF16), preferred_element_type=F32)
            o = pv[g * HEAD_DIM:(g + 1) * HEAD_DIM] / l
            for hh in range(WG_GROUP):
                outs.append(o[:, hh * WINDOW:(hh + 1) * WINDOW])
        o_all = jnp.concatenate(outs, axis=0)
        o_ref[0, qs * WINDOW:(qs + 1) * WINDOW, :] = o_all.T.astype(BF16)


def _window_call(qw, kw, vw, sink_row):
    b, s, _ = kw.shape
    ns = s // TM
    nblk = s // WINDOW
    per = TM // WINDOW
    return pl.pallas_call(
        functools.partial(_window_kernel, seq_len=s),
        grid=(b, ns),
        in_specs=[
            pl.BlockSpec((1, TM // TQ, WG_Q, TQ), lambda bi, c: (bi, c, 0, 0)),
            pl.BlockSpec((1, WINDOW, WG_K), lambda bi, c: (bi, jnp.maximum(c * per - 1, 0), 0)),
            pl.BlockSpec((1, TM, WG_K), lambda bi, c: (bi, c, 0)),
            pl.BlockSpec((1, WINDOW, WG_K), lambda bi, c: (bi, jnp.minimum(c * per + per, nblk - 1), 0)),
            pl.BlockSpec((1, 1, WG_V, WINDOW), lambda bi, c: (bi, jnp.maximum(c - 1, 0), 0, per - 1)),
            pl.BlockSpec((1, 1, WG_V, TM), lambda bi, c: (bi, c, 0, 0)),
            pl.BlockSpec((1, 1, WG_V, WINDOW), lambda bi, c: (bi, jnp.minimum(c + 1, ns - 1), 0, 0)),
            pl.BlockSpec((1, WG_HEADS * WINDOW), lambda bi, c: (0, 0)),
        ],
        out_specs=pl.BlockSpec((1, TM, WG_Q), lambda bi, c: (bi, c, 0)),
        out_shape=jax.ShapeDtypeStruct((b, s, WG_Q), BF16),
        compiler_params=_cparams(("arbitrary", "arbitrary")),
        name="window_gqa",
    )(qw, kw, kw, kw, vw, vw, vw, sink_row)


def _out_kernel(x_ref, oa_ref, ow_ref, mod_ref, gpost_ref, wa_ref, ww_ref, o_ref):
    y = (jnp.dot(oa_ref[0], wa_ref[...], preferred_element_type=F32)
         + jnp.dot(ow_ref[0], ww_ref[...], preferred_element_type=F32))
    o_ref[0] = x_ref[0] + _post(y, mod_ref, gpost_ref, 1)


def _out_call(x, oa, ow, mod, gpost, wa, ww):
    b, s, _ = x.shape
    const = lambda bi, si: (0, 0)
    return pl.pallas_call(
        _out_kernel,
        grid=(b, s // TM),
        in_specs=[
            pl.BlockSpec((1, TM, D_MODEL), lambda bi, si: (bi, si, 0)),
            pl.BlockSpec((1, TM, DA_V), lambda bi, si: (bi, si, 0)),
            pl.BlockSpec((1, TM, WG_Q), lambda bi, si: (bi, si, 0)),
            pl.BlockSpec((1, 3 * N_SUB, D_MODEL), lambda bi, si: (bi, 0, 0)),
            pl.BlockSpec((N_SUB, D_MODEL), const),
            pl.BlockSpec((DA_V, D_MODEL), const),
            pl.BlockSpec((WG_Q, D_MODEL), const),
        ],
        out_specs=pl.BlockSpec((1, TM, D_MODEL), lambda bi, si: (bi, si, 0)),
        out_shape=jax.ShapeDtypeStruct(x.shape, F32),
        compiler_params=_cparams(("arbitrary", "arbitrary")),
        name="out_proj",
    )(x, oa, ow, mod, gpost, wa, ww)


def _rope_tables(seq_len):
    pos = jnp.arange(seq_len, dtype=F32)
    inv_freq = 1.0 / (ROPE_THETA ** (jnp.arange(0, HEAD_DIM, 2, dtype=F32) / HEAD_DIM))
    ang = pos[:, None] * inv_freq[None, :]
    cos = jnp.cos(ang)
    sin = jnp.sin(ang)
    cos_h = jnp.concatenate([cos, cos], axis=-1)
    sin_h = jnp.concatenate([-sin, sin], axis=-1)
    cos_k = jnp.concatenate([cos_h, cos_h], axis=-1)
    sin_k = jnp.concatenate([sin_h, sin_h], axis=-1)
    return cos_h.T, sin_h.T, cos_k, sin_k


def kernel(x_prompt, x_sample, c_prompt, c_sample, w_mod, b_mod, norm_pre, norm_post, w_ff_gate, w_ff_up,
           w_ff_down, w_in, w_out, lambda_q1, lambda_k1, lambda_q2, lambda_k2, sink):
    n_prompt = x_prompt.shape[0]
    c_all = jnp.concatenate([c_prompt, c_sample], axis=0)
    n_seq = c_all.shape[0]
    c_all = jnp.pad(c_all, ((0, -n_seq % SUBLANES), (0, 0)))
    mod_all = _mod_call(c_all, w_mod, b_mod)[:, :n_seq]
    mod_all = mod_all.reshape(DEPTH, n_seq, 3 * N_SUB, D_MODEL)

    wg = w_ff_gate.astype(BF16)
    wu = w_ff_up.astype(BF16)
    wd = w_ff_down.astype(BF16)
    o_qa, o_ka, o_va, o_qw, o_kw, o_vw = 0, DA_Q, DA_Q + DA_K, DA_Q + DA_K + DA_V, \
        DA_Q + DA_K + DA_V + WG_Q, DA_Q + DA_K + DA_V + WG_Q + WG_K
    w_in_b = w_in.astype(BF16)
    wt = jnp.concatenate([w_in_b[:, :, o_qa:o_ka], w_in_b[:, :, o_va:o_qw],
                          w_in_b[:, :, o_qw:o_kw], w_in_b[:, :, o_vw:]], axis=-1).transpose(0, 2, 1)
    wk = jnp.concatenate([w_in_b[:, :, o_ka:o_va], w_in_b[:, :, o_kw:o_vw]], axis=-1)
    w_out_b = w_out.astype(BF16)
    sink_rows = jnp.repeat(sink, WINDOW, axis=-1)

    groups = [(x_prompt, slice(0, n_prompt)), (x_sample, slice(n_prompt, None))]
    tables = {x.shape[1]: _rope_tables(x.shape[1]) for x, _ in groups}
    outs = []
    for x, rows in groups:
        tab = tables[x.shape[1]]
        for l in range(DEPTH):
            mod = mod_all[l, rows]
            lambda_init = 0.8 - 0.6 * math.exp(-0.3 * l)
            x = _ffn_call(x, mod, norm_pre[l], norm_post[l], wg[l, 0], wu[l, 0], wd[l, 0], 0)
            qa, va, qw, vw, ka, kw = _proj_call(x, mod, norm_pre[l], wt[l], wk[l], *tab)
            oa = _diff_call(qa, ka, va, lambda_q1[l:l + 1], lambda_k1[l:l + 1], lambda_q2[l:l + 1],
                            lambda_k2[l:l + 1], lambda_init)
            ow = _window_call(qw, kw, vw, sink_rows[l:l + 1])
            x = _out_call(x, oa, ow, mod, norm_post[l], w_out_b[l, :DA_V], w_out_b[l, DA_V:])
            x = _ffn_call(x, mod, norm_pre[l], norm_post[l], wg[l, 1], wu[l, 1], wd[l, 1], 2)
        outs.append(x)
    return tuple(outs)
```

```python
import functools
import math

import jax
import jax.numpy as jnp
from jax import lax
from jax.experimental import pallas as pl
from jax.experimental.pallas import tpu as pltpu

F32 = jnp.float32
BF16 = jnp.bfloat16

D_MODEL = 1024
DEPTH = 4
HEAD_DIM = 64
HALF = HEAD_DIM // 2
WINDOW = 128
ROPE_THETA = 10000.0
DA_HEADS = 4
DA_VDIM = 2 * HEAD_DIM
WG_HEADS = 8
WG_KV = 2
WG_GROUP = WG_HEADS // WG_KV
DA_Q = DA_HEADS * 2 * HEAD_DIM
DA_K = DA_Q
DA_V = DA_HEADS * DA_VDIM
WG_Q = WG_HEADS * HEAD_DIM
WG_K = WG_KV * HEAD_DIM
WG_V = WG_KV * HEAD_DIM
MIX_WIDTH = DA_V + WG_Q
D_FF = 2816
N_SUB = 3
NORM_EPS = 1e-6
SUBLN_EPS = 1e-5
LOG2E = math.log2(math.e)
Q_SCALE = HEAD_DIM ** -0.5 * LOG2E
BOUND_SLACK = 1.0 + 2.0 ** -12
MIN_COLUMN_SUM = 2.0 ** -80

LANES = 128
SUBLANES = 8
TM = 512
TQ = 256
FF_CHUNK = 256
MOD_TN = 2304
VMEM_LIMIT = 52 * 1024 * 1024


def _cparams(sem):
    return pltpu.CompilerParams(dimension_semantics=sem, vmem_limit_bytes=VMEM_LIMIT)


def _resident(shape):
    return pl.BlockSpec(shape, lambda *_: (0, 0), pipeline_mode=pl.Buffered(1))


def _mod_kernel(c_ref, w_ref, b_ref, o_ref):
    c = c_ref[...]
    a = c * jax.nn.sigmoid(c)
    o_ref[0] = jnp.dot(a, w_ref[0], preferred_element_type=F32,
                       precision=lax.Precision.HIGHEST) + b_ref[0]


def _mod_call(c_all, w_mod, b_mod):
    nb = c_all.shape[0]
    n_out = w_mod.shape[-1]
    return pl.pallas_call(
        _mod_kernel,
        grid=(DEPTH, n_out // MOD_TN),
        in_specs=[
            pl.BlockSpec((nb, D_MODEL), lambda l, n: (0, 0)),
            pl.BlockSpec((1, D_MODEL, MOD_TN), lambda l, n: (l, 0, n)),
            pl.BlockSpec((1, 1, MOD_TN), lambda l, n: (l, 0, n)),
        ],
        out_specs=pl.BlockSpec((1, nb, MOD_TN), lambda l, n: (l, 0, n)),
        out_shape=jax.ShapeDtypeStruct((DEPTH, nb, n_out), F32),
        compiler_params=_cparams(("arbitrary", "arbitrary")),
        name="modulation",
    )(c_all, w_mod, b_mod.reshape(DEPTH, 1, n_out))


def _rms(x, eps):
    return x * lax.rsqrt(jnp.mean(x * x, axis=-1, keepdims=True) + eps)


def _pre(x, mod_ref, gpre_ref, j):
    shift = mod_ref[0, 3 * j:3 * j + 1, :]
    scale = mod_ref[0, 3 * j + 1:3 * j + 2, :]
    hn = _rms(x, NORM_EPS) * gpre_ref[j:j + 1, :]
    return hn * (1.0 + scale) + shift


def _post(y, mod_ref, gpost_ref, j):
    gate = mod_ref[0, 3 * j + 2:3 * j + 3, :]
    return gate * (_rms(y, NORM_EPS) * gpost_ref[j:j + 1, :])


def _ffn(x, mod_ref, gpre_ref, gpost_ref, wg_ref, wu_ref, wd_ref, a_ref, j):
    h = _pre(x, mod_ref, gpre_ref, j).astype(BF16)
    for c in range(D_FF // FF_CHUNK):
        sl = slice(c * FF_CHUNK, (c + 1) * FF_CHUNK)
        g = jnp.dot(h, wg_ref[:, sl], preferred_element_type=F32)
        u = jnp.dot(h, wu_ref[:, sl], preferred_element_type=F32)
        a_ref[:, sl] = (g * jax.nn.sigmoid(g) * u).astype(BF16)
    y = jnp.dot(a_ref[...], wd_ref[...], preferred_element_type=F32)
    return x + 0.5 * _post(y, mod_ref, gpost_ref, j)


def _row_spec(width):
    return pl.BlockSpec((1, TM, width), lambda bi, si: (bi, si, 0))


def _norm_specs():
    const = lambda bi, si: (0, 0)
    return [pl.BlockSpec((1, 3 * N_SUB, D_MODEL), lambda bi, si: (bi, 0, 0)),
            pl.BlockSpec((N_SUB, D_MODEL), const), pl.BlockSpec((N_SUB, D_MODEL), const)]


def _ffn_specs():
    return [_resident((D_MODEL, D_FF)), _resident((D_MODEL, D_FF)), _resident((D_FF, D_MODEL))]


def _rope_rows(y, cos_t, sin_t, n_groups):
    out = []
    for g in range(n_groups):
        blk = y[g * HEAD_DIM:(g + 1) * HEAD_DIM]
        swapped = jnp.concatenate([blk[HALF:], blk[:HALF]], axis=0)
        out.append((blk * cos_t + swapped * sin_t) * Q_SCALE)
    return jnp.concatenate(out, axis=0)


def _ffn_proj_kernel(x_ref, mod_ref, gpre_ref, gpost_ref, wg_ref, wu_ref, wd_ref, wt_ref, wk_ref,
                     cos_t_ref, sin_t_ref, cos_k_ref, sin_k_ref,
                     xo_ref, qa_ref, va_ref, qw_ref, vw_ref, ka_ref, kw_ref, a_ref):
    x = _ffn(x_ref[0], mod_ref, gpre_ref, gpost_ref, wg_ref, wu_ref, wd_ref, a_ref, 0)
    xo_ref[0] = x
    h = _pre(x, mod_ref, gpre_ref, 1).astype(BF16)

    def rows(lo, hi):
        return lax.dot_general(wt_ref[lo:hi, :], h, (((1,), (1,)), ((), ())),
                               preferred_element_type=F32)

    cos_t = cos_t_ref[...]
    sin_t = sin_t_ref[...]
    qa = _rope_rows(rows(0, DA_Q), cos_t, sin_t, DA_Q // HEAD_DIM).astype(BF16)
    qw = _rope_rows(rows(DA_Q + DA_V, DA_Q + DA_V + WG_Q), cos_t, sin_t, WG_Q // HEAD_DIM).astype(BF16)
    for half in range(TM // TQ):
        qa_ref[0, half] = qa[:, half * TQ:(half + 1) * TQ]
        qw_ref[0, half] = qw[:, half * TQ:(half + 1) * TQ]
    va_ref[0, 0] = rows(DA_Q, DA_Q + DA_V).astype(BF16)
    vw_ref[0, 0] = rows(DA_Q + DA_V + WG_Q, DA_Q + DA_V + WG_Q + WG_V).astype(BF16)

    k = jnp.dot(h, wk_ref[...], preferred_element_type=F32)
    cos_k = cos_k_ref[...]
    sin_k = sin_k_ref[...]
    lane = lax.broadcasted_iota(jnp.int32, (TM, LANES), 1)
    first_half = (lane % HEAD_DIM) < HALF
    for c in range((DA_K + WG_K) // LANES):
        blk = k[:, c * LANES:(c + 1) * LANES]
        swapped = jnp.where(first_half, pltpu.roll(blk, LANES - HALF, 1), pltpu.roll(blk, HALF, 1))
        r = (blk * cos_k + swapped * sin_k).astype(BF16)
        if c < DA_K // LANES:
            ka_ref[0, :, c * LANES:(c + 1) * LANES] = r
        else:
            kw_ref[0] = r


def _ffn_proj_call(x, mod, gpre, gpost, wg, wu, wd, wt, wk, cos_t, sin_t, cos_k, sin_k):
    b, s, _ = x.shape
    ns = s // TM
    return pl.pallas_call(
        _ffn_proj_kernel,
        grid=(b, ns),
        in_specs=[_row_spec(D_MODEL)] + _norm_specs() + _ffn_specs() + [
            _resident(wt.shape),
            _resident(wk.shape),
            pl.BlockSpec((HEAD_DIM, TM), lambda bi, si: (0, si)),
            pl.BlockSpec((HEAD_DIM, TM), lambda bi, si: (0, si)),
            pl.BlockSpec((TM, LANES), lambda bi, si: (si, 0)),
            pl.BlockSpec((TM, LANES), lambda bi, si: (si, 0)),
        ],
        out_specs=[
            _row_spec(D_MODEL),
            pl.BlockSpec((1, TM // TQ, DA_Q, TQ), lambda bi, si: (bi, si, 0, 0)),
            pl.BlockSpec((1, 1, DA_V, TM), lambda bi, si: (bi, si, 0, 0)),
            pl.BlockSpec((1, TM // TQ, WG_Q, TQ), lambda bi, si: (bi, si, 0, 0)),
            pl.BlockSpec((1, 1, WG_V, TM), lambda bi, si: (bi, si, 0, 0)),
            pl.BlockSpec((1, TM, DA_K), lambda bi, si: (bi, si, 0)),
            pl.BlockSpec((1, TM, WG_K), lambda bi, si: (bi, si, 0)),
        ],
        out_shape=[
            jax.ShapeDtypeStruct(x.shape, F32),
            jax.ShapeDtypeStruct((b, s // TQ, DA_Q, TQ), BF16),
            jax.ShapeDtypeStruct((b, ns, DA_V, TM), BF16),
            jax.ShapeDtypeStruct((b, s // TQ, WG_Q, TQ), BF16),
            jax.ShapeDtypeStruct((b, ns, WG_V, TM), BF16),
            jax.ShapeDtypeStruct((b, s, DA_K), BF16),
            jax.ShapeDtypeStruct((b, s, WG_K), BF16),
        ],
        scratch_shapes=[pltpu.VMEM((TM, D_FF), BF16)],
        compiler_params=_cparams(("arbitrary", "arbitrary")),
        name="ffn_in_proj",
    )(x, mod, gpre, gpost, wg, wu, wd, wt, wk, cos_t, sin_t, cos_k, sin_k)


def _diff_kernel(q_ref, k_ref, v_ref, lq1_ref, lk1_ref, lq2_ref, lk2_ref, o_ref, knorm_ref, s_ref, *,
                 lambda_init, n_chunks, n_qblocks, unroll_c, unroll_q):
    lam = (jnp.exp(jnp.sum(lq1_ref[...] * lk1_ref[...], axis=-1, keepdims=True))
           - jnp.exp(jnp.sum(lq2_ref[...] * lk2_ref[...], axis=-1, keepdims=True)) + lambda_init)
    row = lax.broadcasted_iota(jnp.int32, (2 * HEAD_DIM, TQ), 0)

    @pl.when(pl.program_id(2) == 0)
    def _():
        def body(j, mx):
            kc = k_ref[0, pl.ds(pl.multiple_of(j * TM, TM), TM), :].astype(F32)
            return jnp.maximum(mx, jnp.sum(kc * kc, axis=-1, keepdims=True))
        mx = lax.fori_loop(0, n_chunks, body, jnp.zeros((TM, 1), F32))
        knorm_ref[...] = jnp.broadcast_to(jnp.sqrt(jnp.max(mx, axis=0, keepdims=True)), knorm_ref.shape)

    def padded_q(qb):
        q = q_ref[0, qb]
        zero = jnp.zeros_like(q)
        return jnp.concatenate([jnp.where(row < HEAD_DIM, q, zero),
                                jnp.where(row >= HEAD_DIM, q, zero)], axis=1)

    def p_times_v(j, pb):
        vt = v_ref[0, j]
        return jnp.concatenate(
            [jnp.dot(vt, pb[:, :TQ], preferred_element_type=F32),
             jnp.dot(vt, pb[:, TQ:], preferred_element_type=F32)], axis=1)

    def finish(qb, l, acc):
        o = acc / l
        o = o[:, :TQ] - lam * o[:, TQ:]
        o = o * lax.rsqrt(jnp.mean(o * o, axis=0, keepdims=True) + SUBLN_EPS) * (1.0 - lambda_init)
        o_ref[0, pl.ds(pl.multiple_of(qb * TQ, TQ), TQ), :] = o.T.astype(BF16)

    def key_chunk(j):
        return k_ref[0, pl.ds(pl.multiple_of(j * TM, TM), TM), :]

    kmax = knorm_ref[0:1, :]
    kmax = jnp.concatenate([kmax] * (2 * TQ // LANES), axis=1)

    def q_block_bounded(qb, l_min):
        qpad = padded_q(qb)
        qf = qpad.astype(F32)
        shift = jnp.sqrt(jnp.sum(qf * qf, axis=0, keepdims=True)) * kmax * BOUND_SLACK

        def logits(j):
            return jnp.dot(key_chunk(j), qpad, preferred_element_type=F32)

        def chunk(j, slot, st, next_j):
            l8, acc = st
            if next_j is not None:
                s_ref[1 - slot] = logits(next_j)
            p = jnp.exp2(s_ref[slot] - shift)
            l8 = l8 + jnp.sum(p.reshape(TM // SUBLANES, SUBLANES, 2 * TQ), axis=0)
            return l8, acc + p_times_v(j, p.astype(BF16))

        st = (jnp.zeros((SUBLANES, 2 * TQ), F32), jnp.zeros((DA_VDIM, 2 * TQ), F32))
        s_ref[0] = logits(0)
        if n_chunks <= unroll_c:
            for j in range(n_chunks):
                st = chunk(j, j % 2, st, j + 1 if j + 1 < n_chunks else None)
        else:
            def group(g, st):
                for i in range(unroll_c):
                    j = g * unroll_c + i
                    st = chunk(j, i % 2, st, jnp.minimum(j + 1, n_chunks - 1))
                return st
            st = lax.fori_loop(0, n_chunks // unroll_c, group, st)
        l8, acc = st
        l = jnp.sum(l8, axis=0, keepdims=True)
        finish(qb, l, acc)
        return jnp.minimum(l_min, l)

    def q_block_online(qb, carry):
        qpad = padded_q(qb)

        def chunk(j, st):
            m, l, acc = st
            s = jnp.dot(key_chunk(j), qpad, preferred_element_type=F32)
            m_new = jnp.maximum(m, jnp.max(s, axis=0, keepdims=True))
            alpha = jnp.exp2(m - m_new)
            p = jnp.exp2(s - m_new)
            l = alpha * l + jnp.sum(p, axis=0, keepdims=True)
            return m_new, l, acc * alpha + p_times_v(j, p.astype(BF16))

        init = (jnp.full((1, 2 * TQ), -jnp.inf, F32), jnp.zeros((1, 2 * TQ), F32),
                jnp.zeros((DA_VDIM, 2 * TQ), F32))
        _, l, acc = lax.fori_loop(0, n_chunks, chunk, init)
        finish(qb, l, acc)
        return carry

    l_min = lax.fori_loop(0, n_qblocks, q_block_bounded, jnp.full((1, 2 * TQ), jnp.inf, F32),
                          unroll=unroll_q)
    bounded_ok = jnp.min(l_min) >= MIN_COLUMN_SUM

    @pl.when(jnp.logical_not(bounded_ok))
    def _():
        lax.fori_loop(0, n_qblocks, q_block_online, 0)


def _diff_call(qa, ka, va, lq1, lk1, lq2, lk2, lambda_init):
    b, s, _ = ka.shape
    tq_outer = min(s, 2048)
    n_qblocks = tq_outer // TQ
    n_chunks = s // TM
    vec = pl.BlockSpec((1, HEAD_DIM), lambda bi, h, qi: (0, 0))
    return pl.pallas_call(
        functools.partial(_diff_kernel, lambda_init=lambda_init, n_chunks=n_chunks, n_qblocks=n_qblocks,
                          unroll_c=min(n_chunks, 8), unroll_q=4 if n_chunks <= 4 else 1),
        grid=(b, DA_HEADS, s // tq_outer),
        in_specs=[
            pl.BlockSpec((1, n_qblocks, 2 * HEAD_DIM, TQ), lambda bi, h, qi: (bi, qi, h, 0)),
            pl.BlockSpec((1, s, 2 * HEAD_DIM), lambda bi, h, qi: (bi, 0, h)),
            pl.BlockSpec((1, n_chunks, DA_VDIM, TM), lambda bi, h, qi: (bi, 0, h, 0)),
            vec, vec, vec, vec,
        ],
        out_specs=pl.BlockSpec((1, tq_outer, DA_VDIM), lambda bi, h, qi: (bi, qi, h)),
        out_shape=jax.ShapeDtypeStruct((b, s, DA_V), BF16),
        scratch_shapes=[pltpu.VMEM((SUBLANES, LANES), F32), pltpu.VMEM((2, TM, 2 * TQ), F32)],
        compiler_params=_cparams(("arbitrary", "arbitrary", "arbitrary")),
        name="diff_attention",
    )(qa, ka, va, lq1, lk1, lq2, lk2)


def _window_kernel(q_ref, kp_ref, kc_ref, kn_ref, vp_ref, vc_ref, vn_ref, sink_ref, o_ref, *, seq_len):
    c = pl.program_id(1)
    keys = jnp.concatenate([kp_ref[0], kc_ref[0], kn_ref[0]], axis=0)
    vals = jnp.concatenate([vp_ref[0, 0], vc_ref[0, 0], vn_ref[0, 0]], axis=1)
    sink = sink_ref[...] * LOG2E
    n_win = 3 * WINDOW
    kidx = lax.broadcasted_iota(jnp.int32, (n_win, WINDOW), 0)
    qidx = lax.broadcasted_iota(jnp.int32, (n_win, WINDOW), 1)
    band = jnp.abs(kidx - WINDOW - qidx) <= WINDOW
    zero_q = jnp.zeros((HEAD_DIM, WINDOW), BF16)

    ksq = keys.astype(F32)
    ksq = ksq * ksq
    lane = lax.broadcasted_iota(jnp.int32, ksq.shape, 1)
    kmax = []
    for g in range(WG_KV):
        mine = (lane >= g * HEAD_DIM) & (lane < (g + 1) * HEAD_DIM)
        n2 = jnp.sum(jnp.where(mine, ksq, 0.0), axis=-1, keepdims=True)
        kmax.append(jnp.sqrt(jnp.max(n2, axis=0, keepdims=True)))

    def attend(bounded):
        l_min = jnp.full((1, WG_GROUP * WINDOW), jnp.inf, F32)
        for qs in range(TM // WINDOW):
            kpos = c * TM + (qs - 1) * WINDOW + kidx
            valid1 = band & (kpos >= 0) & (kpos < seq_len)
            bias1 = jnp.where(valid1, 0.0, -jnp.inf).astype(F32)
            bias = jnp.concatenate([bias1] * WG_GROUP, axis=1)
            kwin = keys[qs * WINDOW:qs * WINDOW + n_win]
            vwin = vals[:, qs * WINDOW:qs * WINDOW + n_win]
            outs = []
            for g in range(WG_KV):
                blocks = []
                for hh in range(WG_GROUP):
                    hd = g * WG_GROUP + hh
                    qh = q_ref[0, qs // 2, hd * HEAD_DIM:(hd + 1) * HEAD_DIM,
                               (qs % 2) * WINDOW:(qs % 2 + 1) * WINDOW]
                    pad = [qh, zero_q] if g == 0 else [zero_q, qh]
                    blocks.append(jnp.concatenate(pad, axis=0))
                qpad = jnp.concatenate(blocks, axis=1)
                s = jnp.dot(kwin, qpad, preferred_element_type=F32) + bias
                sk = sink[:, g * WG_GROUP * WINDOW:(g + 1) * WG_GROUP * WINDOW]
                if bounded:
                    qf = qpad.astype(F32)
                    top = jnp.sqrt(jnp.sum(qf * qf, axis=0, keepdims=True)) * kmax[g] * BOUND_SLACK
                else:
                    top = jnp.max(s, axis=0, keepdims=True)
                shift = jnp.maximum(top, sk)
                p = jnp.exp2(s - shift)
                l = jnp.sum(p, axis=0, keepdims=True) + jnp.exp2(sk - shift)
                l_min = jnp.minimum(l_min, l)
                pv = jnp.dot(vwin, p.astype(BF16), preferred_element_type=F32)
                o = pv[g * HEAD_DIM:(g + 1) * HEAD_DIM] / l
                for hh in range(WG_GROUP):
                    outs.append(o[:, hh * WINDOW:(hh + 1) * WINDOW])
            o_all = jnp.concatenate(outs, axis=0)
            o_ref[0, qs * WINDOW:(qs + 1) * WINDOW, :] = o_all.T.astype(BF16)
        return l_min

    bounded_ok = jnp.min(attend(True)) >= MIN_COLUMN_SUM

    @pl.when(jnp.logical_not(bounded_ok))
    def _():
        attend(False)


def _window_call(qw, kw, vw, sink_row):
    b, s, _ = kw.shape
    ns = s // TM
    nblk = s // WINDOW
    per = TM // WINDOW
    return pl.pallas_call(
        functools.partial(_window_kernel, seq_len=s),
        grid=(b, ns),
        in_specs=[
            pl.BlockSpec((1, TM // TQ, WG_Q, TQ), lambda bi, c: (bi, c, 0, 0)),
            pl.BlockSpec((1, WINDOW, WG_K), lambda bi, c: (bi, jnp.maximum(c * per - 1, 0), 0)),
            pl.BlockSpec((1, TM, WG_K), lambda bi, c: (bi, c, 0)),
            pl.BlockSpec((1, WINDOW, WG_K), lambda bi, c: (bi, jnp.minimum(c * per + per, nblk - 1), 0)),
            pl.BlockSpec((1, 1, WG_V, WINDOW), lambda bi, c: (bi, jnp.maximum(c - 1, 0), 0, per - 1)),
            pl.BlockSpec((1, 1, WG_V, TM), lambda bi, c: (bi, c, 0, 0)),
            pl.BlockSpec((1, 1, WG_V, WINDOW), lambda bi, c: (bi, jnp.minimum(c + 1, ns - 1), 0, 0)),
            pl.BlockSpec((1, WG_HEADS * WINDOW), lambda bi, c: (0, 0)),
        ],
        out_specs=pl.BlockSpec((1, TM, WG_Q), lambda bi, c: (bi, c, 0)),
        out_shape=jax.ShapeDtypeStruct((b, s, WG_Q), BF16),
        compiler_params=_cparams(("arbitrary", "arbitrary")),
        name="window_gqa",
    )(qw, kw, kw, kw, vw, vw, vw, sink_row)


def _out_ffn_kernel(x_ref, oa_ref, ow_ref, mod_ref, gpre_ref, gpost_ref, wa_ref, ww_ref,
                    wg_ref, wu_ref, wd_ref, o_ref, a_ref):
    y = (jnp.dot(oa_ref[0], wa_ref[...], preferred_element_type=F32)
         + jnp.dot(ow_ref[0], ww_ref[...], preferred_element_type=F32))
    x = x_ref[0] + _post(y, mod_ref, gpost_ref, 1)
    o_ref[0] = _ffn(x, mod_ref, gpre_ref, gpost_ref, wg_ref, wu_ref, wd_ref, a_ref, 2)


def _out_ffn_call(x, oa, ow, mod, gpre, gpost, wa, ww, wg, wu, wd):
    b, s, _ = x.shape
    return pl.pallas_call(
        _out_ffn_kernel,
        grid=(b, s // TM),
        in_specs=[_row_spec(D_MODEL), _row_spec(DA_V), _row_spec(WG_Q)] + _norm_specs()
        + [_resident((DA_V, D_MODEL)), _resident((WG_Q, D_MODEL))] + _ffn_specs(),
        out_specs=_row_spec(D_MODEL),
        out_shape=jax.ShapeDtypeStruct(x.shape, F32),
        scratch_shapes=[pltpu.VMEM((TM, D_FF), BF16)],
        compiler_params=_cparams(("arbitrary", "arbitrary")),
        name="out_proj_ffn",
    )(x, oa, ow, mod, gpre, gpost, wa, ww, wg, wu, wd)


def _rope_tables(seq_len):
    pos = jnp.arange(seq_len, dtype=F32)
    inv_freq = 1.0 / (ROPE_THETA ** (jnp.arange(0, HEAD_DIM, 2, dtype=F32) / HEAD_DIM))
    ang = pos[:, None] * inv_freq[None, :]
    cos = jnp.cos(ang)
    sin = jnp.sin(ang)
    cos_h = jnp.concatenate([cos, cos], axis=-1)
    sin_h = jnp.concatenate([-sin, sin], axis=-1)
    cos_k = jnp.concatenate([cos_h, cos_h], axis=-1)
    sin_k = jnp.concatenate([sin_h, sin_h], axis=-1)
    return cos_h.T, sin_h.T, cos_k, sin_k


def kernel(x_prompt, x_sample, c_prompt, c_sample, w_mod, b_mod, norm_pre, norm_post, w_ff_gate, w_ff_up,
           w_ff_down, w_in, w_out, lambda_q1, lambda_k1, lambda_q2, lambda_k2, sink):
    n_prompt = x_prompt.shape[0]
    c_all = jnp.concatenate([c_prompt, c_sample], axis=0)
    n_seq = c_all.shape[0]
    c_all = jnp.pad(c_all, ((0, -n_seq % SUBLANES), (0, 0)))
    mod_all = _mod_call(c_all, w_mod, b_mod)[:, :n_seq]
    mod_all = mod_all.reshape(DEPTH, n_seq, 3 * N_SUB, D_MODEL)

    wg = w_ff_gate.astype(BF16)
    wu = w_ff_up.astype(BF16)
    wd = w_ff_down.astype(BF16)
    o_qa, o_ka, o_va, o_qw, o_kw, o_vw = 0, DA_Q, DA_Q + DA_K, DA_Q + DA_K + DA_V, \
        DA_Q + DA_K + DA_V + WG_Q, DA_Q + DA_K + DA_V + WG_Q + WG_K
    w_in_b = w_in.astype(BF16)
    wt = jnp.concatenate([w_in_b[:, :, o_qa:o_ka], w_in_b[:, :, o_va:o_qw],
                          w_in_b[:, :, o_qw:o_kw], w_in_b[:, :, o_vw:]], axis=-1).transpose(0, 2, 1)
    wk = jnp.concatenate([w_in_b[:, :, o_ka:o_va], w_in_b[:, :, o_kw:o_vw]], axis=-1)
    w_out_b = w_out.astype(BF16)
    sink_rows = jnp.repeat(sink, WINDOW, axis=-1)

    groups = [(x_prompt, slice(0, n_prompt)), (x_sample, slice(n_prompt, None))]
    tables = {x.shape[1]: _rope_tables(x.shape[1]) for x, _ in groups}
    outs = []
    for x, rows in groups:
        tab = tables[x.shape[1]]
        for l in range(DEPTH):
            mod = mod_all[l, rows]
            lambda_init = 0.8 - 0.6 * math.exp(-0.3 * l)
            x, qa, va, qw, vw, ka, kw = _ffn_proj_call(x, mod, norm_pre[l], norm_post[l], wg[l, 0], wu[l, 0],
                                                       wd[l, 0], wt[l], wk[l], *tab)
            oa = _diff_call(qa, ka, va, lambda_q1[l:l + 1], lambda_k1[l:l + 1], lambda_q2[l:l + 1],
                            lambda_k2[l:l + 1], lambda_init)
            ow = _window_call(qw, kw, vw, sink_rows[l:l + 1])
            x = _out_ffn_call(x, oa, ow, mod, norm_pre[l], norm_post[l], w_out_b[l, :DA_V], w_out_b[l, DA_V:],
                              wg[l, 1], wu[l, 1], wd[l, 1])
        outs.append(x)
    return tuple(outs)
```

```python
import functools
import math

import jax
import jax.numpy as jnp
from jax import lax
from jax.experimental import pallas as pl
from jax.experimental.pallas import tpu as pltpu

F32 = jnp.float32
BF16 = jnp.bfloat16

D_MODEL = 1024
DEPTH = 4
HEAD_DIM = 64
HALF = HEAD_DIM // 2
WINDOW = 128
ROPE_THETA = 10000.0
DA_HEADS = 4
DA_VDIM = 2 * HEAD_DIM
WG_HEADS = 8
WG_KV = 2
WG_GROUP = WG_HEADS // WG_KV
DA_Q = DA_HEADS * 2 * HEAD_DIM
DA_K = DA_Q
DA_V = DA_HEADS * DA_VDIM
WG_Q = WG_HEADS * HEAD_DIM
WG_K = WG_KV * HEAD_DIM
WG_V = WG_KV * HEAD_DIM
MIX_WIDTH = DA_V + WG_Q
D_FF = 2816
N_SUB = 3
NORM_EPS = 1e-6
SUBLN_EPS = 1e-5
LOG2E = math.log2(math.e)
Q_SCALE = HEAD_DIM ** -0.5 * LOG2E
BOUND_SLACK = 1.0 + 2.0 ** -12
MIN_COLUMN_SUM = 2.0 ** -80

LANES = 128
SUBLANES = 8
TM = 512
TQ = 256
FF_CHUNK = 256
MOD_TN = 2304
VMEM_LIMIT = 52 * 1024 * 1024


def _cparams(sem):
    return pltpu.CompilerParams(dimension_semantics=sem, vmem_limit_bytes=VMEM_LIMIT)


def _resident(shape):
    return pl.BlockSpec(shape, lambda *_: (0, 0), pipeline_mode=pl.Buffered(1))


def _mod_kernel(c_ref, w_ref, b_ref, o_ref):
    c = c_ref[...]
    a = c * jax.nn.sigmoid(c)
    o_ref[0] = jnp.dot(a, w_ref[0], preferred_element_type=F32,
                       precision=lax.Precision.HIGHEST) + b_ref[0]


def _mod_call(c_all, w_mod, b_mod):
    nb = c_all.shape[0]
    n_out = w_mod.shape[-1]
    return pl.pallas_call(
        _mod_kernel,
        grid=(DEPTH, n_out // MOD_TN),
        in_specs=[
            pl.BlockSpec((nb, D_MODEL), lambda l, n: (0, 0)),
            pl.BlockSpec((1, D_MODEL, MOD_TN), lambda l, n: (l, 0, n)),
            pl.BlockSpec((1, 1, MOD_TN), lambda l, n: (l, 0, n)),
        ],
        out_specs=pl.BlockSpec((1, nb, MOD_TN), lambda l, n: (l, 0, n)),
        out_shape=jax.ShapeDtypeStruct((DEPTH, nb, n_out), F32),
        compiler_params=_cparams(("arbitrary", "arbitrary")),
        name="modulation",
    )(c_all, w_mod, b_mod.reshape(DEPTH, 1, n_out))


def _rms(x, eps):
    return x * lax.rsqrt(jnp.mean(x * x, axis=-1, keepdims=True) + eps)


def _pre(x, mod_ref, gpre_ref, j):
    shift = mod_ref[0, 3 * j:3 * j + 1, :]
    scale = mod_ref[0, 3 * j + 1:3 * j + 2, :]
    hn = _rms(x, NORM_EPS) * gpre_ref[j:j + 1, :]
    return hn * (1.0 + scale) + shift


def _post(y, mod_ref, gpost_ref, j):
    gate = mod_ref[0, 3 * j + 2:3 * j + 3, :]
    return gate * (_rms(y, NORM_EPS) * gpost_ref[j:j + 1, :])


def _ffn(x, mod_ref, gpre_ref, gpost_ref, wg_ref, wu_ref, wd_ref, a_ref, j):
    h = _pre(x, mod_ref, gpre_ref, j).astype(BF16)
    for c in range(D_FF // FF_CHUNK):
        sl = slice(c * FF_CHUNK, (c + 1) * FF_CHUNK)
        g = jnp.dot(h, wg_ref[:, sl], preferred_element_type=F32)
        u = jnp.dot(h, wu_ref[:, sl], preferred_element_type=F32)
        a_ref[:, sl] = (g * jax.nn.sigmoid(g) * u).astype(BF16)
    y = jnp.dot(a_ref[...], wd_ref[...], preferred_element_type=F32)
    return x + 0.5 * _post(y, mod_ref, gpost_ref, j)


def _row_spec(width):
    return pl.BlockSpec((1, TM, width), lambda bi, si: (bi, si, 0))


def _norm_specs():
    const = lambda bi, si: (0, 0)
    return [pl.BlockSpec((1, 3 * N_SUB, D_MODEL), lambda bi, si: (bi, 0, 0)),
            pl.BlockSpec((N_SUB, D_MODEL), const), pl.BlockSpec((N_SUB, D_MODEL), const)]


def _ffn_specs():
    return [_resident((D_MODEL, D_FF)), _resident((D_MODEL, D_FF)), _resident((D_FF, D_MODEL))]


def _rope_rows(y, cos_t, sin_t, n_groups):
    out = []
    for g in range(n_groups):
        blk = y[g * HEAD_DIM:(g + 1) * HEAD_DIM]
        swapped = jnp.concatenate([blk[HALF:], blk[:HALF]], axis=0)
        out.append((blk * cos_t + swapped * sin_t) * Q_SCALE)
    return jnp.concatenate(out, axis=0)


def _ffn_proj_kernel(x_ref, mod_ref, gpre_ref, gpost_ref, wg_ref, wu_ref, wd_ref, wt_ref, wk_ref,
                     cos_t_ref, sin_t_ref, cos_k_ref, sin_k_ref,
                     xo_ref, qa_ref, va_ref, qw_ref, vw_ref, ka_ref, kw_ref, a_ref):
    x = _ffn(x_ref[0], mod_ref, gpre_ref, gpost_ref, wg_ref, wu_ref, wd_ref, a_ref, 0)
    xo_ref[0] = x
    h = _pre(x, mod_ref, gpre_ref, 1).astype(BF16)

    def rows(lo, hi):
        return lax.dot_general(wt_ref[lo:hi, :], h, (((1,), (1,)), ((), ())),
                               preferred_element_type=F32)

    cos_t = cos_t_ref[...]
    sin_t = sin_t_ref[...]
    qa = _rope_rows(rows(0, DA_Q), cos_t, sin_t, DA_Q // HEAD_DIM).astype(BF16)
    qw = _rope_rows(rows(DA_Q + DA_V, DA_Q + DA_V + WG_Q), cos_t, sin_t, WG_Q // HEAD_DIM).astype(BF16)
    for half in range(TM // TQ):
        qa_ref[0, half] = qa[:, half * TQ:(half + 1) * TQ]
        qw_ref[0, half] = qw[:, half * TQ:(half + 1) * TQ]
    va_ref[0, 0] = rows(DA_Q, DA_Q + DA_V).astype(BF16)
    vw_ref[0, 0] = rows(DA_Q + DA_V + WG_Q, DA_Q + DA_V + WG_Q + WG_V).astype(BF16)

    k = jnp.dot(h, wk_ref[...], preferred_element_type=F32)
    cos_k = cos_k_ref[...]
    sin_k = sin_k_ref[...]
    lane = lax.broadcasted_iota(jnp.int32, (TM, LANES), 1)
    first_half = (lane % HEAD_DIM) < HALF
    for c in range((DA_K + WG_K) // LANES):
        blk = k[:, c * LANES:(c + 1) * LANES]
        swapped = jnp.where(first_half, pltpu.roll(blk, LANES - HALF, 1), pltpu.roll(blk, HALF, 1))
        r = (blk * cos_k + swapped * sin_k).astype(BF16)
        if c < DA_K // LANES:
            ka_ref[0, :, c * LANES:(c + 1) * LANES] = r
        else:
            kw_ref[0] = r


def _ffn_proj_call(x, mod, gpre, gpost, wg, wu, wd, wt, wk, cos_t, sin_t, cos_k, sin_k):
    b, s, _ = x.shape
    ns = s // TM
    return pl.pallas_call(
        _ffn_proj_kernel,
        grid=(b, ns),
        in_specs=[_row_spec(D_MODEL)] + _norm_specs() + _ffn_specs() + [
            _resident(wt.shape),
            _resident(wk.shape),
            pl.BlockSpec((HEAD_DIM, TM), lambda bi, si: (0, si)),
            pl.BlockSpec((HEAD_DIM, TM), lambda bi, si: (0, si)),
            pl.BlockSpec((TM, LANES), lambda bi, si: (si, 0)),
            pl.BlockSpec((TM, LANES), lambda bi, si: (si, 0)),
        ],
        out_specs=[
            _row_spec(D_MODEL),
            pl.BlockSpec((1, TM // TQ, DA_Q, TQ), lambda bi, si: (bi, si, 0, 0)),
            pl.BlockSpec((1, 1, DA_V, TM), lambda bi, si: (bi, si, 0, 0)),
            pl.BlockSpec((1, TM // TQ, WG_Q, TQ), lambda bi, si: (bi, si, 0, 0)),
            pl.BlockSpec((1, 1, WG_V, TM), lambda bi, si: (bi, si, 0, 0)),
            pl.BlockSpec((1, TM, DA_K), lambda bi, si: (bi, si, 0)),
            pl.BlockSpec((1, TM, WG_K), lambda bi, si: (bi, si, 0)),
        ],
        out_shape=[
            jax.ShapeDtypeStruct(x.shape, F32),
            jax.ShapeDtypeStruct((b, s // TQ, DA_Q, TQ), BF16),
            jax.ShapeDtypeStruct((b, ns, DA_V, TM), BF16),
            jax.ShapeDtypeStruct((b, s // TQ, WG_Q, TQ), BF16),
            jax.ShapeDtypeStruct((b, ns, WG_V, TM), BF16),
            jax.ShapeDtypeStruct((b, s, DA_K), BF16),
            jax.ShapeDtypeStruct((b, s, WG_K), BF16),
        ],
        scratch_shapes=[pltpu.VMEM((TM, D_FF), BF16)],
        compiler_params=_cparams(("arbitrary", "arbitrary")),
        name="ffn_in_proj",
    )(x, mod, gpre, gpost, wg, wu, wd, wt, wk, cos_t, sin_t, cos_k, sin_k)


def _diff_kernel(q_ref, k_ref, v_ref, lq1_ref, lk1_ref, lq2_ref, lk2_ref, o_ref, knorm_ref, *,
                 lambda_init, n_chunks, n_qblocks, unroll_c, unroll_q):
    lam = (jnp.exp(jnp.sum(lq1_ref[...] * lk1_ref[...], axis=-1, keepdims=True))
           - jnp.exp(jnp.sum(lq2_ref[...] * lk2_ref[...], axis=-1, keepdims=True)) + lambda_init)
    row = lax.broadcasted_iota(jnp.int32, (2 * HEAD_DIM, TQ), 0)

    @pl.when(pl.program_id(2) == 0)
    def _():
        def body(j, mx):
            kc = k_ref[0, pl.ds(pl.multiple_of(j * TM, TM), TM), :].astype(F32)
            return jnp.maximum(mx, jnp.sum(kc * kc, axis=-1, keepdims=True))
        mx = lax.fori_loop(0, n_chunks, body, jnp.zeros((TM, 1), F32))
        knorm_ref[...] = jnp.broadcast_to(jnp.sqrt(jnp.max(mx, axis=0, keepdims=True)), knorm_ref.shape)

    def padded_q(qb):
        q = q_ref[0, qb]
        zero = jnp.zeros_like(q)
        return jnp.concatenate([jnp.where(row < HEAD_DIM, q, zero),
                                jnp.where(row >= HEAD_DIM, q, zero)], axis=1)

    def p_times_v(j, pb):
        vt = v_ref[0, j]
        return jnp.concatenate(
            [jnp.dot(vt, pb[:, :TQ], preferred_element_type=F32),
             jnp.dot(vt, pb[:, TQ:], preferred_element_type=F32)], axis=1)

    def finish(qb, l, acc):
        o = acc / l
        o = o[:, :TQ] - lam * o[:, TQ:]
        o = o * lax.rsqrt(jnp.mean(o * o, axis=0, keepdims=True) + SUBLN_EPS) * (1.0 - lambda_init)
        o_ref[0, pl.ds(pl.multiple_of(qb * TQ, TQ), TQ), :] = o.T.astype(BF16)

    def key_chunk(j):
        return k_ref[0, pl.ds(pl.multiple_of(j * TM, TM), TM), :]

    kmax = knorm_ref[0:1, :]
    kmax = jnp.concatenate([kmax] * (2 * TQ // LANES), axis=1)

    def q_block_bounded(qb, l_min):
        qpad = padded_q(qb)
        qf = qpad.astype(F32)
        shift = jnp.sqrt(jnp.sum(qf * qf, axis=0, keepdims=True)) * kmax * BOUND_SLACK

        def logits(j):
            return jnp.dot(key_chunk(j), qpad, preferred_element_type=F32)

        def run(j0, st):
            l8, acc = st
            s = logits(j0)
            for i in range(unroll_c):
                s_next = logits(j0 + i + 1) if i + 1 < unroll_c else None
                p = jnp.exp2(s - shift)
                l8 = l8 + jnp.sum(p.reshape(TM // SUBLANES, SUBLANES, 2 * TQ), axis=0)
                acc = acc + p_times_v(j0 + i, p.astype(BF16))
                s = s_next
            return l8, acc

        st = (jnp.zeros((SUBLANES, 2 * TQ), F32), jnp.zeros((DA_VDIM, 2 * TQ), F32))
        if n_chunks == unroll_c:
            st = run(0, st)
        else:
            st = lax.fori_loop(0, n_chunks // unroll_c, lambda g, st: run(g * unroll_c, st), st)
        l8, acc = st
        l = jnp.sum(l8, axis=0, keepdims=True)
        finish(qb, l, acc)
        return jnp.minimum(l_min, l)

    def q_block_online(qb, carry):
        qpad = padded_q(qb)

        def chunk(j, st):
            m, l, acc = st
            s = jnp.dot(key_chunk(j), qpad, preferred_element_type=F32)
            m_new = jnp.maximum(m, jnp.max(s, axis=0, keepdims=True))
            alpha = jnp.exp2(m - m_new)
            p = jnp.exp2(s - m_new)
            l = alpha * l + jnp.sum(p, axis=0, keepdims=True)
            return m_new, l, acc * alpha + p_times_v(j, p.astype(BF16))

        init = (jnp.full((1, 2 * TQ), -jnp.inf, F32), jnp.zeros((1, 2 * TQ), F32),
                jnp.zeros((DA_VDIM, 2 * TQ), F32))
        _, l, acc = lax.fori_loop(0, n_chunks, chunk, init)
        finish(qb, l, acc)
        return carry

    l_min = lax.fori_loop(0, n_qblocks, q_block_bounded, jnp.full((1, 2 * TQ), jnp.inf, F32),
                          unroll=unroll_q)
    bounded_ok = jnp.min(l_min) >= MIN_COLUMN_SUM

    @pl.when(jnp.logical_not(bounded_ok))
    def _():
        lax.fori_loop(0, n_qblocks, q_block_online, 0)


def _diff_call(qa, ka, va, lq1, lk1, lq2, lk2, lambda_init):
    b, s, _ = ka.shape
    tq_outer = min(s, 2048)
    n_qblocks = tq_outer // TQ
    n_chunks = s // TM
    vec = pl.BlockSpec((1, HEAD_DIM), lambda bi, h, qi: (0, 0))
    return pl.pallas_call(
        functools.partial(_diff_kernel, lambda_init=lambda_init, n_chunks=n_chunks, n_qblocks=n_qblocks,
                          unroll_c=min(n_chunks, 16), unroll_q=4 if n_chunks <= 4 else 1),
        grid=(b, DA_HEADS, s // tq_outer),
        in_specs=[
            pl.BlockSpec((1, n_qblocks, 2 * HEAD_DIM, TQ), lambda bi, h, qi: (bi, qi, h, 0)),
            pl.BlockSpec((1, s, 2 * HEAD_DIM), lambda bi, h, qi: (bi, 0, h)),
            pl.BlockSpec((1, n_chunks, DA_VDIM, TM), lambda bi, h, qi: (bi, 0, h, 0)),
            vec, vec, vec, vec,
        ],
        out_specs=pl.BlockSpec((1, tq_outer, DA_VDIM), lambda bi, h, qi: (bi, qi, h)),
        out_shape=jax.ShapeDtypeStruct((b, s, DA_V), BF16),
        scratch_shapes=[pltpu.VMEM((SUBLANES, LANES), F32)],
        compiler_params=_cparams(("arbitrary", "arbitrary", "arbitrary")),
        name="diff_attention",
    )(qa, ka, va, lq1, lk1, lq2, lk2)


def _window_kernel(q_ref, kp_ref, kc_ref, kn_ref, vp_ref, vc_ref, vn_ref, sink_ref, o_ref, *, seq_len):
    c = pl.program_id(1)
    keys = jnp.concatenate([kp_ref[0], kc_ref[0], kn_ref[0]], axis=0)
    vals = jnp.concatenate([vp_ref[0, 0], vc_ref[0, 0], vn_ref[0, 0]], axis=1)
    sink = sink_ref[...] * LOG2E
    n_win = 3 * WINDOW
    kidx = lax.broadcasted_iota(jnp.int32, (n_win, WINDOW), 0)
    qidx = lax.broadcasted_iota(jnp.int32, (n_win, WINDOW), 1)
    band = jnp.abs(kidx - WINDOW - qidx) <= WINDOW
    zero_q = jnp.zeros((HEAD_DIM, WINDOW), BF16)

    ksq = keys.astype(F32)
    ksq = ksq * ksq
    lane = lax.broadcasted_iota(jnp.int32, ksq.shape, 1)
    kmax = []
    for g in range(WG_KV):
        mine = (lane >= g * HEAD_DIM) & (lane < (g + 1) * HEAD_DIM)
        n2 = jnp.sum(jnp.where(mine, ksq, 0.0), axis=-1, keepdims=True)
        kmax.append(jnp.sqrt(jnp.max(n2, axis=0, keepdims=True)))

    def attend(bounded):
        def logits(qs, g):
            blocks = []
            for hh in range(WG_GROUP):
                hd = g * WG_GROUP + hh
                qh = q_ref[0, qs // 2, hd * HEAD_DIM:(hd + 1) * HEAD_DIM,
                           (qs % 2) * WINDOW:(qs % 2 + 1) * WINDOW]
                pad = [qh, zero_q] if g == 0 else [zero_q, qh]
                blocks.append(jnp.concatenate(pad, axis=0))
            qpad = jnp.concatenate(blocks, axis=1)
            kwin = keys[qs * WINDOW:qs * WINDOW + n_win]
            return qpad, jnp.dot(kwin, qpad, preferred_element_type=F32)

        tiles = [(qs, g) for qs in range(TM // WINDOW) for g in range(WG_KV)]
        l_min = jnp.full((1, WG_GROUP * WINDOW), jnp.inf, F32)
        outs = []
        nxt = logits(*tiles[0])
        for t, (qs, g) in enumerate(tiles):
            qpad, s = nxt
            if t + 1 < len(tiles):
                nxt = logits(*tiles[t + 1])
            kpos = c * TM + (qs - 1) * WINDOW + kidx
            valid1 = band & (kpos >= 0) & (kpos < seq_len)
            bias1 = jnp.where(valid1, 0.0, -jnp.inf).astype(F32)
            s = s + jnp.concatenate([bias1] * WG_GROUP, axis=1)
            vwin = vals[:, qs * WINDOW:qs * WINDOW + n_win]
            sk = sink[:, g * WG_GROUP * WINDOW:(g + 1) * WG_GROUP * WINDOW]
            if bounded:
                qf = qpad.astype(F32)
                top = jnp.sqrt(jnp.sum(qf * qf, axis=0, keepdims=True)) * kmax[g] * BOUND_SLACK
            else:
                top = jnp.max(s, axis=0, keepdims=True)
            shift = jnp.maximum(top, sk)
            p = jnp.exp2(s - shift)
            l = jnp.sum(p, axis=0, keepdims=True) + jnp.exp2(sk - shift)
            l_min = jnp.minimum(l_min, l)
            pv = jnp.dot(vwin, p.astype(BF16), preferred_element_type=F32)
            o = pv[g * HEAD_DIM:(g + 1) * HEAD_DIM] / l
            for hh in range(WG_GROUP):
                outs.append(o[:, hh * WINDOW:(hh + 1) * WINDOW])
            if g == WG_KV - 1:
                o_all = jnp.concatenate(outs, axis=0)
                o_ref[0, qs * WINDOW:(qs + 1) * WINDOW, :] = o_all.T.astype(BF16)
                outs = []
        return l_min

    bounded_ok = jnp.min(attend(True)) >= MIN_COLUMN_SUM

    @pl.when(jnp.logical_not(bounded_ok))
    def _():
        attend(False)


def _window_call(qw, kw, vw, sink_row):
    b, s, _ = kw.shape
    ns = s // TM
    nblk = s // WINDOW
    per = TM // WINDOW
    return pl.pallas_call(
        functools.partial(_window_kernel, seq_len=s),
        grid=(b, ns),
        in_specs=[
            pl.BlockSpec((1, TM // TQ, WG_Q, TQ), lambda bi, c: (bi, c, 0, 0)),
            pl.BlockSpec((1, WINDOW, WG_K), lambda bi, c: (bi, jnp.maximum(c * per - 1, 0), 0)),
            pl.BlockSpec((1, TM, WG_K), lambda bi, c: (bi, c, 0)),
            pl.BlockSpec((1, WINDOW, WG_K), lambda bi, c: (bi, jnp.minimum(c * per + per, nblk - 1), 0)),
            pl.BlockSpec((1, 1, WG_V, WINDOW), lambda bi, c: (bi, jnp.maximum(c - 1, 0), 0, per - 1)),
            pl.BlockSpec((1, 1, WG_V, TM), lambda bi, c: (bi, c, 0, 0)),
            pl.BlockSpec((1, 1, WG_V, WINDOW), lambda bi, c: (bi, jnp.minimum(c + 1, ns - 1), 0, 0)),
            pl.BlockSpec((1, WG_HEADS * WINDOW), lambda bi, c: (0, 0)),
        ],
        out_specs=pl.BlockSpec((1, TM, WG_Q), lambda bi, c: (bi, c, 0)),
        out_shape=jax.ShapeDtypeStruct((b, s, WG_Q), BF16),
        compiler_params=_cparams(("arbitrary", "arbitrary")),
        name="window_gqa",
    )(qw, kw, kw, kw, vw, vw, vw, sink_row)


def _out_ffn_kernel(x_ref, oa_ref, ow_ref, mod_ref, gpre_ref, gpost_ref, wa_ref, ww_ref,
                    wg_ref, wu_ref, wd_ref, o_ref, a_ref):
    y = (jnp.dot(oa_ref[0], wa_ref[...], preferred_element_type=F32)
         + jnp.dot(ow_ref[0], ww_ref[...], preferred_element_type=F32))
    x = x_ref[0] + _post(y, mod_ref, gpost_ref, 1)
    o_ref[0] = _ffn(x, mod_ref, gpre_ref, gpost_ref, wg_ref, wu_ref, wd_ref, a_ref, 2)


def _out_ffn_call(x, oa, ow, mod, gpre, gpost, wa, ww, wg, wu, wd):
    b, s, _ = x.shape
    return pl.pallas_call(
        _out_ffn_kernel,
        grid=(b, s // TM),
        in_specs=[_row_spec(D_MODEL), _row_spec(DA_V), _row_spec(WG_Q)] + _norm_specs()
        + [_resident((DA_V, D_MODEL)), _resident((WG_Q, D_MODEL))] + _ffn_specs(),
        out_specs=_row_spec(D_MODEL),
        out_shape=jax.ShapeDtypeStruct(x.shape, F32),
        scratch_shapes=[pltpu.VMEM((TM, D_FF), BF16)],
        compiler_params=_cparams(("arbitrary", "arbitrary")),
        name="out_proj_ffn",
    )(x, oa, ow, mod, gpre, gpost, wa, ww, wg, wu, wd)


def _rope_tables(seq_len):
    pos = jnp.arange(seq_len, dtype=F32)
    inv_freq = 1.0 / (ROPE_THETA ** (jnp.arange(0, HEAD_DIM, 2, dtype=F32) / HEAD_DIM))
    ang = pos[:, None] * inv_freq[None, :]
    cos = jnp.cos(ang)
    sin = jnp.sin(ang)
    cos_h = jnp.concatenate([cos, cos], axis=-1)
    sin_h = jnp.concatenate([-sin, sin], axis=-1)
    cos_k = jnp.concatenate([cos_h, cos_h], axis=-1)
    sin_k = jnp.concatenate([sin_h, sin_h], axis=-1)
    return cos_h.T, sin_h.T, cos_k, sin_k


def kernel(x_prompt, x_sample, c_prompt, c_sample, w_mod, b_mod, norm_pre, norm_post, w_ff_gate, w_ff_up,
           w_ff_down, w_in, w_out, lambda_q1, lambda_k1, lambda_q2, lambda_k2, sink):
    n_prompt = x_prompt.shape[0]
    c_all = jnp.concatenate([c_prompt, c_sample], axis=0)
    n_seq = c_all.shape[0]
    c_all = jnp.pad(c_all, ((0, -n_seq % SUBLANES), (0, 0)))
    mod_all = _mod_call(c_all, w_mod, b_mod)[:, :n_seq]
    mod_all = mod_all.reshape(DEPTH, n_seq, 3 * N_SUB, D_MODEL)

    wg = w_ff_gate.astype(BF16)
    wu = w_ff_up.astype(BF16)
    wd = w_ff_down.astype(BF16)
    o_qa, o_ka, o_va, o_qw, o_kw, o_vw = 0, DA_Q, DA_Q + DA_K, DA_Q + DA_K + DA_V, \
        DA_Q + DA_K + DA_V + WG_Q, DA_Q + DA_K + DA_V + WG_Q + WG_K
    w_in_b = w_in.astype(BF16)
    wt = jnp.concatenate([w_in_b[:, :, o_qa:o_ka], w_in_b[:, :, o_va:o_qw],
                          w_in_b[:, :, o_qw:o_kw], w_in_b[:, :, o_vw:]], axis=-1).transpose(0, 2, 1)
    wk = jnp.concatenate([w_in_b[:, :, o_ka:o_va], w_in_b[:, :, o_kw:o_vw]], axis=-1)
    w_out_b = w_out.astype(BF16)
    sink_rows = jnp.repeat(sink, WINDOW, axis=-1)

    groups = [(x_prompt, slice(0, n_prompt)), (x_sample, slice(n_prompt, None))]
    tables = {x.shape[1]: _rope_tables(x.shape[1]) for x, _ in groups}
    outs = []
    for x, rows in groups:
        tab = tables[x.shape[1]]
        for l in range(DEPTH):
            mod = mod_all[l, rows]
            lambda_init = 0.8 - 0.6 * math.exp(-0.3 * l)
            x, qa, va, qw, vw, ka, kw = _ffn_proj_call(x, mod, norm_pre[l], norm_post[l], wg[l, 0], wu[l, 0],
                                                       wd[l, 0], wt[l], wk[l], *tab)
            oa = _diff_call(qa, ka, va, lambda_q1[l:l + 1], lambda_k1[l:l + 1], lambda_q2[l:l + 1],
                            lambda_k2[l:l + 1], lambda_init)
            ow = _window_call(qw, kw, vw, sink_rows[l:l + 1])
            x = _out_ffn_call(x, oa, ow, mod, norm_pre[l], norm_post[l], w_out_b[l, :DA_V], w_out_b[l, DA_V:],
                              wg[l, 1], wu[l, 1], wd[l, 1])
        outs.append(x)
    return tuple(outs)
```

```python
import functools
import math

import jax
import jax.numpy as jnp
from jax import lax
from jax.experimental import pallas as pl
from jax.experimental.pallas import tpu as pltpu

F32 = jnp.float32
BF16 = jnp.bfloat16

D_MODEL = 1024
DEPTH = 4
HEAD_DIM = 64
HALF = HEAD_DIM // 2
WINDOW = 128
ROPE_THETA = 10000.0
DA_HEADS = 4
DA_VDIM = 2 * HEAD_DIM
WG_HEADS = 8
WG_KV = 2
WG_GROUP = WG_HEADS // WG_KV
DA_Q = DA_HEADS * 2 * HEAD_DIM
DA_K = DA_Q
DA_V = DA_HEADS * DA_VDIM
WG_Q = WG_HEADS * HEAD_DIM
WG_K = WG_KV * HEAD_DIM
WG_V = WG_KV * HEAD_DIM
MIX_WIDTH = DA_V + WG_Q
D_FF = 2816
N_SUB = 3
NORM_EPS = 1e-6
SUBLN_EPS = 1e-5
LOG2E = math.log2(math.e)
Q_SCALE = HEAD_DIM ** -0.5 * LOG2E
BOUND_SLACK = 1.0 + 2.0 ** -12
MIN_COLUMN_SUM = 2.0 ** -80

LANES = 128
SUBLANES = 8
TM = 512
TQ = 256
FF_CHUNK = 256
MOD_TN = 2304
VMEM_LIMIT = 52 * 1024 * 1024


def _cparams(sem):
    return pltpu.CompilerParams(dimension_semantics=sem, vmem_limit_bytes=VMEM_LIMIT)


def _resident(shape):
    return pl.BlockSpec(shape, lambda *_: (0, 0), pipeline_mode=pl.Buffered(1))


def _mod_kernel(c_ref, w_ref, b_ref, o_ref):
    c = c_ref[...]
    a = c * jax.nn.sigmoid(c)
    o_ref[0] = jnp.dot(a, w_ref[0], preferred_element_type=F32,
                       precision=lax.Precision.HIGHEST) + b_ref[0]


def _mod_call(c_all, w_mod, b_mod):
    nb = c_all.shape[0]
    n_out = w_mod.shape[-1]
    return pl.pallas_call(
        _mod_kernel,
        grid=(DEPTH, n_out // MOD_TN),
        in_specs=[
            pl.BlockSpec((nb, D_MODEL), lambda l, n: (0, 0)),
            pl.BlockSpec((1, D_MODEL, MOD_TN), lambda l, n: (l, 0, n)),
            pl.BlockSpec((1, 1, MOD_TN), lambda l, n: (l, 0, n)),
        ],
        out_specs=pl.BlockSpec((1, nb, MOD_TN), lambda l, n: (l, 0, n)),
        out_shape=jax.ShapeDtypeStruct((DEPTH, nb, n_out), F32),
        compiler_params=_cparams(("arbitrary", "arbitrary")),
        name="modulation",
    )(c_all, w_mod, b_mod.reshape(DEPTH, 1, n_out))


def _rms(x, eps):
    return x * lax.rsqrt(jnp.mean(x * x, axis=-1, keepdims=True) + eps)


def _pre(x, mod_ref, gpre_ref, j):
    shift = mod_ref[0, 3 * j:3 * j + 1, :]
    scale = mod_ref[0, 3 * j + 1:3 * j + 2, :]
    return _rms(x, NORM_EPS) * (gpre_ref[j:j + 1, :] * (1.0 + scale)) + shift


def _post(y, mod_ref, gpost_ref, j, coef=1.0):
    gate = mod_ref[0, 3 * j + 2:3 * j + 3, :]
    return _rms(y, NORM_EPS) * (coef * gate * gpost_ref[j:j + 1, :])


HALF_ROWS = [pl.ds(r * (TM // 2), TM // 2) for r in range(2)]


def _gate_up(h_ref, a_ref, wg_ref, wu_ref, rows):
    for c in range(D_FF // FF_CHUNK):
        sl = slice(c * FF_CHUNK, (c + 1) * FF_CHUNK)
        g = jnp.dot(h_ref[rows], wg_ref[:, sl], preferred_element_type=F32)
        u = jnp.dot(h_ref[rows], wu_ref[:, sl], preferred_element_type=F32)
        a_ref[rows, sl] = (g * jax.nn.sigmoid(g) * u).astype(BF16)


def _down(a_ref, wd_ref, rows):
    return jnp.dot(a_ref[rows], wd_ref[...], preferred_element_type=F32)


def _row_spec(width):
    return pl.BlockSpec((1, TM, width), lambda bi, si: (bi, si, 0))


def _norm_specs():
    const = lambda bi, si: (0, 0)
    return [pl.BlockSpec((1, 3 * N_SUB, D_MODEL), lambda bi, si: (bi, 0, 0)),
            pl.BlockSpec((N_SUB, D_MODEL), const), pl.BlockSpec((N_SUB, D_MODEL), const)]


def _ffn_specs():
    return [_resident((D_MODEL, D_FF)), _resident((D_MODEL, D_FF)), _resident((D_FF, D_MODEL))]


def _rope_rows(y, cos_t, sin_t, n_groups):
    out = []
    for g in range(n_groups):
        blk = y[g * HEAD_DIM:(g + 1) * HEAD_DIM]
        swapped = jnp.concatenate([blk[HALF:], blk[:HALF]], axis=0)
        out.append((blk * cos_t + swapped * sin_t) * Q_SCALE)
    return jnp.concatenate(out, axis=0)


def _ffn_proj_kernel(x_ref, mod_ref, gpre_ref, gpost_ref, wg_ref, wu_ref, wd_ref, wt_ref, wk_ref,
                     cos_t_ref, sin_t_ref, cos_k_ref, sin_k_ref,
                     xo_ref, qa_ref, va_ref, qw_ref, vw_ref, ka_ref, kw_ref, a_ref, h_ref):
    hm = TM // 2
    lane = lax.broadcasted_iota(jnp.int32, (hm, LANES), 1)
    first_half = (lane % HEAD_DIM) < HALF

    def head(r):
        rows = HALF_ROWS[r]
        h_ref[rows] = _pre(x_ref[0, rows], mod_ref, gpre_ref, 0).astype(BF16)

    def mid(r, y):
        rows = HALF_ROWS[r]
        x = x_ref[0, rows] + _post(y, mod_ref, gpost_ref, 0, coef=0.5)
        xo_ref[0, rows] = x
        h_ref[rows] = _pre(x, mod_ref, gpre_ref, 1).astype(BF16)

    def project(r):
        rows = HALF_ROWS[r]
        cols = slice(r * hm, (r + 1) * hm)
        h = h_ref[rows]

        k = jnp.dot(h, wk_ref[...], preferred_element_type=F32)
        cos_k = cos_k_ref[rows, :]
        sin_k = sin_k_ref[rows, :]
        for c in range((DA_K + WG_K) // LANES):
            blk = k[:, c * LANES:(c + 1) * LANES]
            swapped = jnp.where(first_half, pltpu.roll(blk, LANES - HALF, 1), pltpu.roll(blk, HALF, 1))
            kr = (blk * cos_k + swapped * sin_k).astype(BF16)
            if c < DA_K // LANES:
                ka_ref[0, rows, c * LANES:(c + 1) * LANES] = kr
            else:
                kw_ref[0, rows, :] = kr

        def t_rows(lo, hi):
            return lax.dot_general(wt_ref[lo:hi, :], h, (((1,), (1,)), ((), ())),
                                   preferred_element_type=F32)

        cos_t = cos_t_ref[:, cols]
        sin_t = sin_t_ref[:, cols]
        qa_ref[0, r] = _rope_rows(t_rows(0, DA_Q), cos_t, sin_t, DA_Q // HEAD_DIM).astype(BF16)
        qw_ref[0, r] = _rope_rows(t_rows(DA_Q + DA_V, DA_Q + DA_V + WG_Q), cos_t, sin_t,
                                  WG_Q // HEAD_DIM).astype(BF16)
        va_ref[0, 0, :, cols] = t_rows(DA_Q, DA_Q + DA_V).astype(BF16)
        vw_ref[0, 0, :, cols] = t_rows(DA_Q + DA_V + WG_Q, DA_Q + DA_V + WG_Q + WG_V).astype(BF16)

    head(0)
    _gate_up(h_ref, a_ref, wg_ref, wu_ref, HALF_ROWS[0])
    head(1)
    y0 = _down(a_ref, wd_ref, HALF_ROWS[0])
    _gate_up(h_ref, a_ref, wg_ref, wu_ref, HALF_ROWS[1])
    mid(0, y0)
    y1 = _down(a_ref, wd_ref, HALF_ROWS[1])
    project(0)
    mid(1, y1)
    project(1)


def _ffn_proj_call(x, mod, gpre, gpost, wg, wu, wd, wt, wk, cos_t, sin_t, cos_k, sin_k):
    b, s, _ = x.shape
    ns = s // TM
    return pl.pallas_call(
        _ffn_proj_kernel,
        grid=(b, ns),
        in_specs=[_row_spec(D_MODEL)] + _norm_specs() + _ffn_specs() + [
            _resident(wt.shape),
            _resident(wk.shape),
            pl.BlockSpec((HEAD_DIM, TM), lambda bi, si: (0, si)),
            pl.BlockSpec((HEAD_DIM, TM), lambda bi, si: (0, si)),
            pl.BlockSpec((TM, LANES), lambda bi, si: (si, 0)),
            pl.BlockSpec((TM, LANES), lambda bi, si: (si, 0)),
        ],
        out_specs=[
            _row_spec(D_MODEL),
            pl.BlockSpec((1, TM // TQ, DA_Q, TQ), lambda bi, si: (bi, si, 0, 0)),
            pl.BlockSpec((1, 1, DA_V, TM), lambda bi, si: (bi, si, 0, 0)),
            pl.BlockSpec((1, TM // TQ, WG_Q, TQ), lambda bi, si: (bi, si, 0, 0)),
            pl.BlockSpec((1, 1, WG_V, TM), lambda bi, si: (bi, si, 0, 0)),
            pl.BlockSpec((1, TM, DA_K), lambda bi, si: (bi, si, 0)),
            pl.BlockSpec((1, TM, WG_K), lambda bi, si: (bi, si, 0)),
        ],
        out_shape=[
            jax.ShapeDtypeStruct(x.shape, F32),
            jax.ShapeDtypeStruct((b, s // TQ, DA_Q, TQ), BF16),
            jax.ShapeDtypeStruct((b, ns, DA_V, TM), BF16),
            jax.ShapeDtypeStruct((b, s // TQ, WG_Q, TQ), BF16),
            jax.ShapeDtypeStruct((b, ns, WG_V, TM), BF16),
            jax.ShapeDtypeStruct((b, s, DA_K), BF16),
            jax.ShapeDtypeStruct((b, s, WG_K), BF16),
        ],
        scratch_shapes=[pltpu.VMEM((TM, D_FF), BF16), pltpu.VMEM((TM, D_MODEL), BF16)],
        compiler_params=_cparams(("arbitrary", "arbitrary")),
        name="ffn_in_proj",
    )(x, mod, gpre, gpost, wg, wu, wd, wt, wk, cos_t, sin_t, cos_k, sin_k)


def _diff_kernel(q_ref, k_ref, v_ref, lq1_ref, lk1_ref, lq2_ref, lk2_ref, o_ref, knorm_ref, *,
                 lambda_init, n_chunks, n_qblocks, unroll_c, unroll_q):
    lam = (jnp.exp(jnp.sum(lq1_ref[...] * lk1_ref[...], axis=-1, keepdims=True))
           - jnp.exp(jnp.sum(lq2_ref[...] * lk2_ref[...], axis=-1, keepdims=True)) + lambda_init)
    row = lax.broadcasted_iota(jnp.int32, (2 * HEAD_DIM, TQ), 0)

    @pl.when(pl.program_id(2) == 0)
    def _():
        def body(j, mx):
            kc = k_ref[0, pl.ds(pl.multiple_of(j * TM, TM), TM), :].astype(F32)
            return jnp.maximum(mx, jnp.sum(kc * kc, axis=-1, keepdims=True))
        mx = lax.fori_loop(0, n_chunks, body, jnp.zeros((TM, 1), F32))
        knorm_ref[...] = jnp.broadcast_to(jnp.sqrt(jnp.max(mx, axis=0, keepdims=True)), knorm_ref.shape)

    def padded_q(qb):
        q = q_ref[0, qb]
        zero = jnp.zeros_like(q)
        return jnp.concatenate([jnp.where(row < HEAD_DIM, q, zero),
                                jnp.where(row >= HEAD_DIM, q, zero)], axis=1)

    def p_times_v(j, pb):
        vt = v_ref[0, j]
        return jnp.concatenate(
            [jnp.dot(vt, pb[:, :TQ], preferred_element_type=F32),
             jnp.dot(vt, pb[:, TQ:], preferred_element_type=F32)], axis=1)

    def finish(qb, l, acc):
        o = acc / l
        o = o[:, :TQ] - lam * o[:, TQ:]
        o = o * lax.rsqrt(jnp.mean(o * o, axis=0, keepdims=True) + SUBLN_EPS) * (1.0 - lambda_init)
        o_ref[0, pl.ds(pl.multiple_of(qb * TQ, TQ), TQ), :] = o.T.astype(BF16)

    def key_chunk(j):
        return k_ref[0, pl.ds(pl.multiple_of(j * TM, TM), TM), :]

    kmax = knorm_ref[0:1, :]
    kmax = jnp.concatenate([kmax] * (2 * TQ // LANES), axis=1)

    def q_block_bounded(qb, l_min):
        qpad = padded_q(qb)
        qf = qpad.astype(F32)
        shift = jnp.sqrt(jnp.sum(qf * qf, axis=0, keepdims=True)) * kmax * BOUND_SLACK

        def logits(j):
            return jnp.dot(key_chunk(j), qpad, preferred_element_type=F32)

        def run(j0, st):
            l8, acc = st
            s = logits(j0)
            for i in range(unroll_c):
                s_next = logits(j0 + i + 1) if i + 1 < unroll_c else None
                p = jnp.exp2(s - shift)
                l8 = l8 + jnp.sum(p.reshape(TM // SUBLANES, SUBLANES, 2 * TQ), axis=0)
                acc = acc + p_times_v(j0 + i, p.astype(BF16))
                s = s_next
            return l8, acc

        st = (jnp.zeros((SUBLANES, 2 * TQ), F32), jnp.zeros((DA_VDIM, 2 * TQ), F32))
        if n_chunks == unroll_c:
            st = run(0, st)
        else:
            st = lax.fori_loop(0, n_chunks // unroll_c, lambda g, st: run(g * unroll_c, st), st)
        l8, acc = st
        l = jnp.sum(l8, axis=0, keepdims=True)
        finish(qb, l, acc)
        return jnp.minimum(l_min, l)

    def q_block_online(qb, carry):
        qpad = padded_q(qb)

        def chunk(j, st):
            m, l, acc = st
            s = jnp.dot(key_chunk(j), qpad, preferred_element_type=F32)
            m_new = jnp.maximum(m, jnp.max(s, axis=0, keepdims=True))
            alpha = jnp.exp2(m - m_new)
            p = jnp.exp2(s - m_new)
            l = alpha * l + jnp.sum(p, axis=0, keepdims=True)
            return m_new, l, acc * alpha + p_times_v(j, p.astype(BF16))

        init = (jnp.full((1, 2 * TQ), -jnp.inf, F32), jnp.zeros((1, 2 * TQ), F32),
                jnp.zeros((DA_VDIM, 2 * TQ), F32))
        _, l, acc = lax.fori_loop(0, n_chunks, chunk, init)
        finish(qb, l, acc)
        return carry

    l_min = lax.fori_loop(0, n_qblocks, q_block_bounded, jnp.full((1, 2 * TQ), jnp.inf, F32),
                          unroll=unroll_q)
    bounded_ok = jnp.min(l_min) >= MIN_COLUMN_SUM

    @pl.when(jnp.logical_not(bounded_ok))
    def _():
        lax.fori_loop(0, n_qblocks, q_block_online, 0)


def _diff_call(qa, ka, va, lq1, lk1, lq2, lk2, lambda_init):
    b, s, _ = ka.shape
    tq_outer = min(s, 2048)
    n_qblocks = tq_outer // TQ
    n_chunks = s // TM
    vec = pl.BlockSpec((1, HEAD_DIM), lambda bi, h, qi: (0, 0))
    return pl.pallas_call(
        functools.partial(_diff_kernel, lambda_init=lambda_init, n_chunks=n_chunks, n_qblocks=n_qblocks,
                          unroll_c=min(n_chunks, 32), unroll_q=4 if n_chunks <= 4 else 1),
        grid=(b, DA_HEADS, s // tq_outer),
        in_specs=[
            pl.BlockSpec((1, n_qblocks, 2 * HEAD_DIM, TQ), lambda bi, h, qi: (bi, qi, h, 0)),
            pl.BlockSpec((1, s, 2 * HEAD_DIM), lambda bi, h, qi: (bi, 0, h)),
            pl.BlockSpec((1, n_chunks, DA_VDIM, TM), lambda bi, h, qi: (bi, 0, h, 0)),
            vec, vec, vec, vec,
        ],
        out_specs=pl.BlockSpec((1, tq_outer, DA_VDIM), lambda bi, h, qi: (bi, qi, h)),
        out_shape=jax.ShapeDtypeStruct((b, s, DA_V), BF16),
        scratch_shapes=[pltpu.VMEM((SUBLANES, LANES), F32)],
        compiler_params=_cparams(("arbitrary", "arbitrary", "arbitrary")),
        name="diff_attention",
    )(qa, ka, va, lq1, lk1, lq2, lk2)


def _window_kernel(q_ref, kp_ref, kc_ref, kn_ref, vp_ref, vc_ref, vn_ref, sink_ref, o_ref, *, seq_len):
    c = pl.program_id(1)
    keys = jnp.concatenate([kp_ref[0], kc_ref[0], kn_ref[0]], axis=0)
    vals = jnp.concatenate([vp_ref[0, 0], vc_ref[0, 0], vn_ref[0, 0]], axis=1)
    sink = sink_ref[...] * LOG2E
    n_win = 3 * WINDOW
    kidx = lax.broadcasted_iota(jnp.int32, (n_win, WINDOW), 0)
    qidx = lax.broadcasted_iota(jnp.int32, (n_win, WINDOW), 1)
    band = jnp.abs(kidx - WINDOW - qidx) <= WINDOW
    zero_q = jnp.zeros((HEAD_DIM, WINDOW), BF16)

    ksq = keys.astype(F32)
    ksq = ksq * ksq
    lane = lax.broadcasted_iota(jnp.int32, ksq.shape, 1)
    kmax = []
    for g in range(WG_KV):
        mine = (lane >= g * HEAD_DIM) & (lane < (g + 1) * HEAD_DIM)
        n2 = jnp.sum(jnp.where(mine, ksq, 0.0), axis=-1, keepdims=True)
        kmax.append(jnp.sqrt(jnp.max(n2, axis=0, keepdims=True)))

    def attend(bounded):
        def logits(qs, g):
            blocks = []
            for hh in range(WG_GROUP):
                hd = g * WG_GROUP + hh
                qh = q_ref[0, qs // 2, hd * HEAD_DIM:(hd + 1) * HEAD_DIM,
                           (qs % 2) * WINDOW:(qs % 2 + 1) * WINDOW]
                pad = [qh, zero_q] if g == 0 else [zero_q, qh]
                blocks.append(jnp.concatenate(pad, axis=0))
            qpad = jnp.concatenate(blocks, axis=1)
            kwin = keys[qs * WINDOW:qs * WINDOW + n_win]
            return qpad, jnp.dot(kwin, qpad, preferred_element_type=F32)

        tiles = [(qs, g) for qs in range(TM // WINDOW) for g in range(WG_KV)]
        l_min = jnp.full((1, WG_GROUP * WINDOW), jnp.inf, F32)
        outs = []
        nxt = logits(*tiles[0])
        for t, (qs, g) in enumerate(tiles):
            qpad, s = nxt
            if t + 1 < len(tiles):
                nxt = logits(*tiles[t + 1])
            kpos = c * TM + (qs - 1) * WINDOW + kidx
            valid1 = band & (kpos >= 0) & (kpos < seq_len)
            bias1 = jnp.where(valid1, 0.0, -jnp.inf).astype(F32)
            s = s + jnp.concatenate([bias1] * WG_GROUP, axis=1)
            vwin = vals[:, qs * WINDOW:qs * WINDOW + n_win]
            sk = sink[:, g * WG_GROUP * WINDOW:(g + 1) * WG_GROUP * WINDOW]
            if bounded:
                qf = qpad.astype(F32)
                top = jnp.sqrt(jnp.sum(qf * qf, axis=0, keepdims=True)) * kmax[g] * BOUND_SLACK
            else:
                top = jnp.max(s, axis=0, keepdims=True)
            shift = jnp.maximum(top, sk)
            p = jnp.exp2(s - shift)
            l = jnp.sum(p, axis=0, keepdims=True) + jnp.exp2(sk - shift)
            l_min = jnp.minimum(l_min, l)
            pv = jnp.dot(vwin, p.astype(BF16), preferred_element_type=F32)
            o = pv[g * HEAD_DIM:(g + 1) * HEAD_DIM] / l
            for hh in range(WG_GROUP):
                outs.append(o[:, hh * WINDOW:(hh + 1) * WINDOW])
            if g == WG_KV - 1:
                o_all = jnp.concatenate(outs, axis=0)
                o_ref[0, qs * WINDOW:(qs + 1) * WINDOW, :] = o_all.T.astype(BF16)
                outs = []
        return l_min

    bounded_ok = jnp.min(attend(True)) >= MIN_COLUMN_SUM

    @pl.when(jnp.logical_not(bounded_ok))
    def _():
        attend(False)


def _window_call(qw, kw, vw, sink_row):
    b, s, _ = kw.shape
    ns = s // TM
    nblk = s // WINDOW
    per = TM // WINDOW
    return pl.pallas_call(
        functools.partial(_window_kernel, seq_len=s),
        grid=(b, ns),
        in_specs=[
            pl.BlockSpec((1, TM // TQ, WG_Q, TQ), lambda bi, c: (bi, c, 0, 0)),
            pl.BlockSpec((1, WINDOW, WG_K), lambda bi, c: (bi, jnp.maximum(c * per - 1, 0), 0)),
            pl.BlockSpec((1, TM, WG_K), lambda bi, c: (bi, c, 0)),
            pl.BlockSpec((1, WINDOW, WG_K), lambda bi, c: (bi, jnp.minimum(c * per + per, nblk - 1), 0)),
            pl.BlockSpec((1, 1, WG_V, WINDOW), lambda bi, c: (bi, jnp.maximum(c - 1, 0), 0, per - 1)),
            pl.BlockSpec((1, 1, WG_V, TM), lambda bi, c: (bi, c, 0, 0)),
            pl.BlockSpec((1, 1, WG_V, WINDOW), lambda bi, c: (bi, jnp.minimum(c + 1, ns - 1), 0, 0)),
            pl.BlockSpec((1, WG_HEADS * WINDOW), lambda bi, c: (0, 0)),
        ],
        out_specs=pl.BlockSpec((1, TM, WG_Q), lambda bi, c: (bi, c, 0)),
        out_shape=jax.ShapeDtypeStruct((b, s, WG_Q), BF16),
        compiler_params=_cparams(("arbitrary", "arbitrary")),
        name="window_gqa",
    )(qw, kw, kw, kw, vw, vw, vw, sink_row)


def _out_ffn_kernel(x_ref, oa_ref, ow_ref, mod_ref, gpre_ref, gpost_ref, wa_ref, ww_ref,
                    wg_ref, wu_ref, wd_ref, o_ref, a_ref, h_ref):
    def head(r):
        rows = HALF_ROWS[r]
        y = (jnp.dot(oa_ref[0, rows], wa_ref[...], preferred_element_type=F32)
             + jnp.dot(ow_ref[0, rows], ww_ref[...], preferred_element_type=F32))
        x = x_ref[0, rows] + _post(y, mod_ref, gpost_ref, 1)
        o_ref[0, rows] = x
        h_ref[rows] = _pre(x, mod_ref, gpre_ref, 2).astype(BF16)

    def tail(r, y):
        rows = HALF_ROWS[r]
        o_ref[0, rows] = o_ref[0, rows] + _post(y, mod_ref, gpost_ref, 2, coef=0.5)

    head(0)
    _gate_up(h_ref, a_ref, wg_ref, wu_ref, HALF_ROWS[0])
    head(1)
    y0 = _down(a_ref, wd_ref, HALF_ROWS[0])
    _gate_up(h_ref, a_ref, wg_ref, wu_ref, HALF_ROWS[1])
    tail(0, y0)
    y1 = _down(a_ref, wd_ref, HALF_ROWS[1])
    tail(1, y1)


def _out_ffn_call(x, oa, ow, mod, gpre, gpost, wa, ww, wg, wu, wd):
    b, s, _ = x.shape
    return pl.pallas_call(
        _out_ffn_kernel,
        grid=(b, s // TM),
        in_specs=[_row_spec(D_MODEL), _row_spec(DA_V), _row_spec(WG_Q)] + _norm_specs()
        + [_resident((DA_V, D_MODEL)), _resident((WG_Q, D_MODEL))] + _ffn_specs(),
        out_specs=_row_spec(D_MODEL),
        out_shape=jax.ShapeDtypeStruct(x.shape, F32),
        scratch_shapes=[pltpu.VMEM((TM, D_FF), BF16), pltpu.VMEM((TM, D_MODEL), BF16)],
        compiler_params=_cparams(("arbitrary", "arbitrary")),
        name="out_proj_ffn",
    )(x, oa, ow, mod, gpre, gpost, wa, ww, wg, wu, wd)


def _rope_tables(seq_len):
    pos = jnp.arange(seq_len, dtype=F32)
    inv_freq = 1.0 / (ROPE_THETA ** (jnp.arange(0, HEAD_DIM, 2, dtype=F32) / HEAD_DIM))
    ang = pos[:, None] * inv_freq[None, :]
    cos = jnp.cos(ang)
    sin = jnp.sin(ang)
    cos_h = jnp.concatenate([cos, cos], axis=-1)
    sin_h = jnp.concatenate([-sin, sin], axis=-1)
    cos_k = jnp.concatenate([cos_h, cos_h], axis=-1)
    sin_k = jnp.concatenate([sin_h, sin_h], axis=-1)
    return cos_h.T, sin_h.T, cos_k, sin_k


def kernel(x_prompt, x_sample, c_prompt, c_sample, w_mod, b_mod, norm_pre, norm_post, w_ff_gate, w_ff_up,
           w_ff_down, w_in, w_out, lambda_q1, lambda_k1, lambda_q2, lambda_k2, sink):
    n_prompt = x_prompt.shape[0]
    c_all = jnp.concatenate([c_prompt, c_sample], axis=0)
    n_seq = c_all.shape[0]
    c_all = jnp.pad(c_all, ((0, -n_seq % SUBLANES), (0, 0)))
    mod_all = _mod_call(c_all, w_mod, b_mod)[:, :n_seq]
    mod_all = mod_all.reshape(DEPTH, n_seq, 3 * N_SUB, D_MODEL)

    wg = w_ff_gate.astype(BF16)
    wu = w_ff_up.astype(BF16)
    wd = w_ff_down.astype(BF16)
    o_qa, o_ka, o_va, o_qw, o_kw, o_vw = 0, DA_Q, DA_Q + DA_K, DA_Q + DA_K + DA_V, \
        DA_Q + DA_K + DA_V + WG_Q, DA_Q + DA_K + DA_V + WG_Q + WG_K
    w_in_b = w_in.astype(BF16)
    wt = jnp.concatenate([w_in_b[:, :, o_qa:o_ka], w_in_b[:, :, o_va:o_qw],
                          w_in_b[:, :, o_qw:o_kw], w_in_b[:, :, o_vw:]], axis=-1).transpose(0, 2, 1)
    wk = jnp.concatenate([w_in_b[:, :, o_ka:o_va], w_in_b[:, :, o_kw:o_vw]], axis=-1)
    w_out_b = w_out.astype(BF16)
    sink_rows = jnp.repeat(sink, WINDOW, axis=-1)

    groups = [(x_prompt, slice(0, n_prompt)), (x_sample, slice(n_prompt, None))]
    tables = {x.shape[1]: _rope_tables(x.shape[1]) for x, _ in groups}
    outs = []
    for x, rows in groups:
        tab = tables[x.shape[1]]
        for l in range(DEPTH):
            mod = mod_all[l, rows]
            lambda_init = 0.8 - 0.6 * math.exp(-0.3 * l)
            x, qa, va, qw, vw, ka, kw = _ffn_proj_call(x, mod, norm_pre[l], norm_post[l], wg[l, 0], wu[l, 0],
                                                       wd[l, 0], wt[l], wk[l], *tab)
            oa = _diff_call(qa, ka, va, lambda_q1[l:l + 1], lambda_k1[l:l + 1], lambda_q2[l:l + 1],
                            lambda_k2[l:l + 1], lambda_init)
            ow = _window_call(qw, kw, vw, sink_rows[l:l + 1])
            x = _out_ffn_call(x, oa, ow, mod, norm_pre[l], norm_post[l], w_out_b[l, :DA_V], w_out_b[l, DA_V:],
                              wg[l, 1], wu[l, 1], wd[l, 1])
        outs.append(x)
    return tuple(outs)
```

```python
import functools
import math

import jax
import jax.numpy as jnp
from jax import lax
from jax.experimental import pallas as pl
from jax.experimental.pallas import tpu as pltpu

F32 = jnp.float32
BF16 = jnp.bfloat16

D_MODEL = 1024
DEPTH = 4
HEAD_DIM = 64
HALF = HEAD_DIM // 2
WINDOW = 128
ROPE_THETA = 10000.0
DA_HEADS = 4
DA_VDIM = 2 * HEAD_DIM
WG_HEADS = 8
WG_KV = 2
WG_GROUP = WG_HEADS // WG_KV
DA_Q = DA_HEADS * 2 * HEAD_DIM
DA_K = DA_Q
DA_V = DA_HEADS * DA_VDIM
WG_Q = WG_HEADS * HEAD_DIM
WG_K = WG_KV * HEAD_DIM
WG_V = WG_KV * HEAD_DIM
MIX_WIDTH = DA_V + WG_Q
D_FF = 2816
N_SUB = 3
NORM_EPS = 1e-6
SUBLN_EPS = 1e-5
LOG2E = math.log2(math.e)
Q_SCALE = HEAD_DIM ** -0.5 * LOG2E
BOUND_SLACK = 1.0 + 2.0 ** -12
MIN_COLUMN_SUM = 2.0 ** -80

LANES = 128
SUBLANES = 8
TM = 512
TQ = 256
FF_CHUNK = 256
MOD_TN = 2304
VMEM_LIMIT = 56 * 1024 * 1024


def _cparams(sem):
    return pltpu.CompilerParams(dimension_semantics=sem, vmem_limit_bytes=VMEM_LIMIT)


def _resident(shape):
    return pl.BlockSpec(shape, lambda *_: (0, 0), pipeline_mode=pl.Buffered(1))


def _mod_kernel(c_ref, w_ref, b_ref, o_ref):
    c = c_ref[...]
    a = c * jax.nn.sigmoid(c)
    o_ref[0] = jnp.dot(a, w_ref[0], preferred_element_type=F32,
                       precision=lax.Precision.HIGHEST) + b_ref[0]


def _mod_call(c_all, w_mod, b_mod):
    nb = c_all.shape[0]
    n_out = w_mod.shape[-1]
    return pl.pallas_call(
        _mod_kernel,
        grid=(DEPTH, n_out // MOD_TN),
        in_specs=[
            pl.BlockSpec((nb, D_MODEL), lambda l, n: (0, 0)),
            pl.BlockSpec((1, D_MODEL, MOD_TN), lambda l, n: (l, 0, n)),
            pl.BlockSpec((1, 1, MOD_TN), lambda l, n: (l, 0, n)),
        ],
        out_specs=pl.BlockSpec((1, nb, MOD_TN), lambda l, n: (l, 0, n)),
        out_shape=jax.ShapeDtypeStruct((DEPTH, nb, n_out), F32),
        compiler_params=_cparams(("arbitrary", "arbitrary")),
        name="modulation",
    )(c_all, w_mod, b_mod.reshape(DEPTH, 1, n_out))


def _rms(x, eps):
    return x * lax.rsqrt(jnp.mean(x * x, axis=-1, keepdims=True) + eps)


def _pre(x, mod_ref, gpre_ref, j):
    shift = mod_ref[0, 3 * j:3 * j + 1, :]
    scale = mod_ref[0, 3 * j + 1:3 * j + 2, :]
    return _rms(x, NORM_EPS) * (gpre_ref[j:j + 1, :] * (1.0 + scale)) + shift


def _post(y, mod_ref, gpost_ref, j, coef=1.0):
    gate = mod_ref[0, 3 * j + 2:3 * j + 3, :]
    return _rms(y, NORM_EPS) * (coef * gate * gpost_ref[j:j + 1, :])


SUB_ROWS = TM // 2
HALF_ROWS = [pl.ds(r * SUB_ROWS, SUB_ROWS) for r in range(2)]
OUT_FFN_TILES = 2
FFN_PROJ_TILES = 2
WIN_TILES = 2


def _sub_rows(r):
    return pl.ds(r * SUB_ROWS, SUB_ROWS)


def _slot(r):
    return HALF_ROWS[r % 2]


def _gate_up(h_ref, a_ref, wg_ref, wu_ref, rows):
    for c in range(D_FF // FF_CHUNK):
        sl = slice(c * FF_CHUNK, (c + 1) * FF_CHUNK)
        g = jnp.dot(h_ref[rows], wg_ref[:, sl], preferred_element_type=F32)
        u = jnp.dot(h_ref[rows], wu_ref[:, sl], preferred_element_type=F32)
        a_ref[rows, sl] = (g * jax.nn.sigmoid(g) * u).astype(BF16)


def _down(a_ref, wd_ref, rows):
    return jnp.dot(a_ref[rows], wd_ref[...], preferred_element_type=F32)


def _row_spec(width, tiles=1):
    return pl.BlockSpec((1, tiles * TM, width), lambda bi, si: (bi, si, 0))


def _norm_specs():
    const = lambda bi, si: (0, 0)
    return [pl.BlockSpec((1, 3 * N_SUB, D_MODEL), lambda bi, si: (bi, 0, 0)),
            pl.BlockSpec((N_SUB, D_MODEL), const), pl.BlockSpec((N_SUB, D_MODEL), const)]


def _ffn_specs():
    return [_resident((D_MODEL, D_FF)), _resident((D_MODEL, D_FF)), _resident((D_FF, D_MODEL))]


def _rope_rows(y, cos_t, sin_t, n_groups):
    out = []
    for g in range(n_groups):
        blk = y[g * HEAD_DIM:(g + 1) * HEAD_DIM]
        swapped = jnp.concatenate([blk[HALF:], blk[:HALF]], axis=0)
        out.append((blk * cos_t + swapped * sin_t) * Q_SCALE)
    return jnp.concatenate(out, axis=0)


def _ffn_proj_kernel(x_ref, mod_ref, gpre_ref, gpost_ref, wg_ref, wu_ref, wd_ref, wt_ref, wk_ref,
                     cos_t_ref, sin_t_ref, cos_k_ref, sin_k_ref,
                     xo_ref, qa_ref, va_ref, qw_ref, vw_ref, ka_ref, kw_ref, a_ref, h_ref, hmix_ref):
    hm = SUB_ROWS
    n_sub = x_ref.shape[1] // SUB_ROWS
    lane = lax.broadcasted_iota(jnp.int32, (hm, LANES), 1)
    first_half = (lane % HEAD_DIM) < HALF

    def head(r):
        h_ref[_slot(r)] = _pre(x_ref[0, _sub_rows(r)], mod_ref, gpre_ref, 0).astype(BF16)

    def mid(r, y):
        rows = _sub_rows(r)
        x = x_ref[0, rows] + _post(y, mod_ref, gpost_ref, 0, coef=0.5)
        xo_ref[0, rows] = x
        hmix_ref[_slot(r)] = _pre(x, mod_ref, gpre_ref, 1).astype(BF16)

    def project(r):
        rows = _sub_rows(r)
        tile, half = divmod(r, TM // SUB_ROWS)
        cols = slice(half * hm, (half + 1) * hm)
        h = hmix_ref[_slot(r)]

        k = jnp.dot(h, wk_ref[...], preferred_element_type=F32)
        cos_k = cos_k_ref[rows, :]
        sin_k = sin_k_ref[rows, :]
        for c in range((DA_K + WG_K) // LANES):
            blk = k[:, c * LANES:(c + 1) * LANES]
            swapped = jnp.where(first_half, pltpu.roll(blk, LANES - HALF, 1), pltpu.roll(blk, HALF, 1))
            kr = (blk * cos_k + swapped * sin_k).astype(BF16)
            if c < DA_K // LANES:
                ka_ref[0, rows, c * LANES:(c + 1) * LANES] = kr
            else:
                kw_ref[0, rows, :] = kr

        def t_rows(lo, hi):
            return lax.dot_general(wt_ref[lo:hi, :], h, (((1,), (1,)), ((), ())),
                                   preferred_element_type=F32)

        tok = slice(r * hm, (r + 1) * hm)
        cos_t = cos_t_ref[:, tok]
        sin_t = sin_t_ref[:, tok]
        qa_ref[0, r] = _rope_rows(t_rows(0, DA_Q), cos_t, sin_t, DA_Q // HEAD_DIM).astype(BF16)
        qw_ref[0, r] = _rope_rows(t_rows(DA_Q + DA_V, DA_Q + DA_V + WG_Q), cos_t, sin_t,
                                  WG_Q // HEAD_DIM).astype(BF16)
        va_ref[0, tile, :, cols] = t_rows(DA_Q, DA_Q + DA_V).astype(BF16)
        vw_ref[0, tile, :, cols] = t_rows(DA_Q + DA_V + WG_Q, DA_Q + DA_V + WG_Q + WG_V).astype(BF16)

    head(0)
    _gate_up(h_ref, a_ref, wg_ref, wu_ref, _slot(0))
    for r in range(1, n_sub):
        head(r)
        y = _down(a_ref, wd_ref, _slot(r - 1))
        if r >= 2:
            project(r - 2)
        _gate_up(h_ref, a_ref, wg_ref, wu_ref, _slot(r))
        mid(r - 1, y)
    y = _down(a_ref, wd_ref, _slot(n_sub - 1))
    project(n_sub - 2)
    mid(n_sub - 1, y)
    project(n_sub - 1)


def _ffn_proj_call(x, mod, gpre, gpost, wg, wu, wd, wt, wk, cos_t, sin_t, cos_k, sin_k):
    b, s, _ = x.shape
    ns = s // TM
    tiles = FFN_PROJ_TILES
    return pl.pallas_call(
        _ffn_proj_kernel,
        grid=(b, ns // tiles),
        in_specs=[_row_spec(D_MODEL, tiles)] + _norm_specs() + _ffn_specs() + [
            _resident(wt.shape),
            _resident(wk.shape),
            pl.BlockSpec((HEAD_DIM, tiles * TM), lambda bi, si: (0, si)),
            pl.BlockSpec((HEAD_DIM, tiles * TM), lambda bi, si: (0, si)),
            pl.BlockSpec((tiles * TM, LANES), lambda bi, si: (si, 0)),
            pl.BlockSpec((tiles * TM, LANES), lambda bi, si: (si, 0)),
        ],
        out_specs=[
            _row_spec(D_MODEL, tiles),
            pl.BlockSpec((1, tiles * TM // TQ, DA_Q, TQ), lambda bi, si: (bi, si, 0, 0)),
            pl.BlockSpec((1, tiles, DA_V, TM), lambda bi, si: (bi, si, 0, 0)),
            pl.BlockSpec((1, tiles * TM // TQ, WG_Q, TQ), lambda bi, si: (bi, si, 0, 0)),
            pl.BlockSpec((1, tiles, WG_V, TM), lambda bi, si: (bi, si, 0, 0)),
            _row_spec(DA_K, tiles),
            _row_spec(WG_K, tiles),
        ],
        out_shape=[
            jax.ShapeDtypeStruct(x.shape, F32),
            jax.ShapeDtypeStruct((b, s // TQ, DA_Q, TQ), BF16),
            jax.ShapeDtypeStruct((b, ns, DA_V, TM), BF16),
            jax.ShapeDtypeStruct((b, s // TQ, WG_Q, TQ), BF16),
            jax.ShapeDtypeStruct((b, ns, WG_V, TM), BF16),
            jax.ShapeDtypeStruct((b, s, DA_K), BF16),
            jax.ShapeDtypeStruct((b, s, WG_K), BF16),
        ],
        scratch_shapes=[pltpu.VMEM((TM, D_FF), BF16), pltpu.VMEM((TM, D_MODEL), BF16),
                        pltpu.VMEM((TM, D_MODEL), BF16)],
        compiler_params=_cparams(("arbitrary", "arbitrary")),
        name="ffn_in_proj",
    )(x, mod, gpre, gpost, wg, wu, wd, wt, wk, cos_t, sin_t, cos_k, sin_k)


def _diff_kernel(q_ref, k_ref, v_ref, lq1_ref, lk1_ref, lq2_ref, lk2_ref, o_ref, knorm_ref, *,
                 lambda_init, n_chunks, n_qblocks, unroll_c, unroll_q):
    lam = (jnp.exp(jnp.sum(lq1_ref[...] * lk1_ref[...], axis=-1, keepdims=True))
           - jnp.exp(jnp.sum(lq2_ref[...] * lk2_ref[...], axis=-1, keepdims=True)) + lambda_init)
    row = lax.broadcasted_iota(jnp.int32, (2 * HEAD_DIM, TQ), 0)

    @pl.when(pl.program_id(2) == 0)
    def _():
        def body(j, mx):
            kc = k_ref[0, pl.ds(pl.multiple_of(j * TM, TM), TM), :].astype(F32)
            return jnp.maximum(mx, jnp.sum(kc * kc, axis=-1, keepdims=True))
        mx = lax.fori_loop(0, n_chunks, body, jnp.zeros((TM, 1), F32))
        knorm_ref[...] = jnp.broadcast_to(jnp.sqrt(jnp.max(mx, axis=0, keepdims=True)), knorm_ref.shape)

    def padded_q(qb):
        q = q_ref[0, qb]
        zero = jnp.zeros_like(q)
        return jnp.concatenate([jnp.where(row < HEAD_DIM, q, zero),
                                jnp.where(row >= HEAD_DIM, q, zero)], axis=1)

    def p_times_v(j, pb):
        vt = v_ref[0, j]
        return jnp.concatenate(
            [jnp.dot(vt, pb[:, :TQ], preferred_element_type=F32),
             jnp.dot(vt, pb[:, TQ:], preferred_element_type=F32)], axis=1)

    def finish(qb, l, acc):
        o = acc / l
        o = o[:, :TQ] - lam * o[:, TQ:]
        o = o * lax.rsqrt(jnp.mean(o * o, axis=0, keepdims=True) + SUBLN_EPS) * (1.0 - lambda_init)
        o_ref[0, pl.ds(pl.multiple_of(qb * TQ, TQ), TQ), :] = o.T.astype(BF16)

    def key_chunk(j):
        return k_ref[0, pl.ds(pl.multiple_of(j * TM, TM), TM), :]

    kmax = knorm_ref[0:1, :]
    kmax = jnp.concatenate([kmax] * (2 * TQ // LANES), axis=1)

    def q_block_bounded(qb, l_min):
        qpad = padded_q(qb)
        qf = qpad.astype(F32)
        shift = jnp.sqrt(jnp.sum(qf * qf, axis=0, keepdims=True)) * kmax * BOUND_SLACK

        def logits(j):
            return jnp.dot(key_chunk(j), qpad, preferred_element_type=F32)

        def run(j0, st):
            l8, acc = st
            s = logits(j0)
            for i in range(unroll_c):
                s_next = logits(j0 + i + 1) if i + 1 < unroll_c else None
                p = jnp.exp2(s - shift)
                l8 = l8 + jnp.sum(p.reshape(TM // SUBLANES, SUBLANES, 2 * TQ), axis=0)
                acc = acc + p_times_v(j0 + i, p.astype(BF16))
                s = s_next
            return l8, acc

        st = (jnp.zeros((SUBLANES, 2 * TQ), F32), jnp.zeros((DA_VDIM, 2 * TQ), F32))
        if n_chunks == unroll_c:
            st = run(0, st)
        else:
            st = lax.fori_loop(0, n_chunks // unroll_c, lambda g, st: run(g * unroll_c, st), st)
        l8, acc = st
        l = jnp.sum(l8, axis=0, keepdims=True)
        finish(qb, l, acc)
        return jnp.minimum(l_min, l)

    def q_block_online(qb, carry):
        qpad = padded_q(qb)

        def chunk(j, st):
            m, l, acc = st
            s = jnp.dot(key_chunk(j), qpad, preferred_element_type=F32)
            m_new = jnp.maximum(m, jnp.max(s, axis=0, keepdims=True))
            alpha = jnp.exp2(m - m_new)
            p = jnp.exp2(s - m_new)
            l = alpha * l + jnp.sum(p, axis=0, keepdims=True)
            return m_new, l, acc * alpha + p_times_v(j, p.astype(BF16))

        init = (jnp.full((1, 2 * TQ), -jnp.inf, F32), jnp.zeros((1, 2 * TQ), F32),
                jnp.zeros((DA_VDIM, 2 * TQ), F32))
        _, l, acc = lax.fori_loop(0, n_chunks, chunk, init)
        finish(qb, l, acc)
        return carry

    l_min = lax.fori_loop(0, n_qblocks, q_block_bounded, jnp.full((1, 2 * TQ), jnp.inf, F32),
                          unroll=unroll_q)
    bounded_ok = jnp.min(l_min) >= MIN_COLUMN_SUM

    @pl.when(jnp.logical_not(bounded_ok))
    def _():
        lax.fori_loop(0, n_qblocks, q_block_online, 0)


def _diff_call(qa, ka, va, lq1, lk1, lq2, lk2, lambda_init):
    b, s, _ = ka.shape
    tq_outer = min(s, 2048)
    n_qblocks = tq_outer // TQ
    n_chunks = s // TM
    vec = pl.BlockSpec((1, HEAD_DIM), lambda bi, h, qi: (0, 0))
    return pl.pallas_call(
        functools.partial(_diff_kernel, lambda_init=lambda_init, n_chunks=n_chunks, n_qblocks=n_qblocks,
                          unroll_c=min(n_chunks, 32), unroll_q=n_qblocks if n_chunks <= 4 else 1),
        grid=(b, DA_HEADS, s // tq_outer),
        in_specs=[
            pl.BlockSpec((1, n_qblocks, 2 * HEAD_DIM, TQ), lambda bi, h, qi: (bi, qi, h, 0)),
            pl.BlockSpec((1, s, 2 * HEAD_DIM), lambda bi, h, qi: (bi, 0, h)),
            pl.BlockSpec((1, n_chunks, DA_VDIM, TM), lambda bi, h, qi: (bi, 0, h, 0)),
            vec, vec, vec, vec,
        ],
        out_specs=pl.BlockSpec((1, tq_outer, DA_VDIM), lambda bi, h, qi: (bi, qi, h)),
        out_shape=jax.ShapeDtypeStruct((b, s, DA_V), BF16),
        scratch_shapes=[pltpu.VMEM((SUBLANES, LANES), F32)],
        compiler_params=_cparams(("arbitrary", "arbitrary", "arbitrary")),
        name="diff_attention",
    )(qa, ka, va, lq1, lk1, lq2, lk2)


def _window_kernel(q_ref, kp_ref, kc_ref, kn_ref, vp_ref, vc_ref, vn_ref, sink_ref, o_ref, *, seq_len):
    c = pl.program_id(1)
    step = kc_ref.shape[1]
    keys = jnp.concatenate([kp_ref[0], kc_ref[0], kn_ref[0]], axis=0)
    vals = jnp.concatenate([vp_ref[0, 0]] + [vc_ref[0, t] for t in range(vc_ref.shape[1])]
                           + [vn_ref[0, 0]], axis=1)
    sink = sink_ref[...] * LOG2E
    n_win = 3 * WINDOW
    kidx = lax.broadcasted_iota(jnp.int32, (n_win, WINDOW), 0)
    qidx = lax.broadcasted_iota(jnp.int32, (n_win, WINDOW), 1)
    band = jnp.abs(kidx - WINDOW - qidx) <= WINDOW
    zero_q = jnp.zeros((HEAD_DIM, WINDOW), BF16)

    ksq = keys.astype(F32)
    ksq = ksq * ksq
    lane = lax.broadcasted_iota(jnp.int32, ksq.shape, 1)
    kmax = []
    for g in range(WG_KV):
        mine = (lane >= g * HEAD_DIM) & (lane < (g + 1) * HEAD_DIM)
        n2 = jnp.sum(jnp.where(mine, ksq, 0.0), axis=-1, keepdims=True)
        kmax.append(jnp.sqrt(jnp.max(n2, axis=0, keepdims=True)))

    def attend(bounded):
        def logits(qs, g):
            blocks = []
            for hh in range(WG_GROUP):
                hd = g * WG_GROUP + hh
                qh = q_ref[0, qs // 2, hd * HEAD_DIM:(hd + 1) * HEAD_DIM,
                           (qs % 2) * WINDOW:(qs % 2 + 1) * WINDOW]
                pad = [qh, zero_q] if g == 0 else [zero_q, qh]
                blocks.append(jnp.concatenate(pad, axis=0))
            qpad = jnp.concatenate(blocks, axis=1)
            kwin = keys[qs * WINDOW:qs * WINDOW + n_win]
            return qpad, jnp.dot(kwin, qpad, preferred_element_type=F32)

        tiles = [(qs, g) for qs in range(step // WINDOW) for g in range(WG_KV)]
        l_min = jnp.full((1, WG_GROUP * WINDOW), jnp.inf, F32)
        outs = []
        nxt = logits(*tiles[0])
        for t, (qs, g) in enumerate(tiles):
            qpad, s = nxt
            if t + 1 < len(tiles):
                nxt = logits(*tiles[t + 1])
            kpos = c * step + (qs - 1) * WINDOW + kidx
            valid1 = band & (kpos >= 0) & (kpos < seq_len)
            bias1 = jnp.where(valid1, 0.0, -jnp.inf).astype(F32)
            s = s + jnp.concatenate([bias1] * WG_GROUP, axis=1)
            vwin = vals[:, qs * WINDOW:qs * WINDOW + n_win]
            sk = sink[:, g * WG_GROUP * WINDOW:(g + 1) * WG_GROUP * WINDOW]
            if bounded:
                qf = qpad.astype(F32)
                top = jnp.sqrt(jnp.sum(qf * qf, axis=0, keepdims=True)) * kmax[g] * BOUND_SLACK
            else:
                top = jnp.max(s, axis=0, keepdims=True)
            shift = jnp.maximum(top, sk)
            p = jnp.exp2(s - shift)
            l = jnp.sum(p, axis=0, keepdims=True) + jnp.exp2(sk - shift)
            l_min = jnp.minimum(l_min, l)
            pv = jnp.dot(vwin, p.astype(BF16), preferred_element_type=F32)
            o = pv[g * HEAD_DIM:(g + 1) * HEAD_DIM] / l
            for hh in range(WG_GROUP):
                outs.append(o[:, hh * WINDOW:(hh + 1) * WINDOW])
            if g == WG_KV - 1:
                o_all = jnp.concatenate(outs, axis=0)
                o_ref[0, qs * WINDOW:(qs + 1) * WINDOW, :] = o_all.T.astype(BF16)
                outs = []
        return l_min

    bounded_ok = jnp.min(attend(True)) >= MIN_COLUMN_SUM

    @pl.when(jnp.logical_not(bounded_ok))
    def _():
        attend(False)


def _window_call(qw, kw, vw, sink_row):
    b, s, _ = kw.shape
    ns = s // TM
    nblk = s // WINDOW
    step = WIN_TILES * TM
    per = step // WINDOW
    last = TM // WINDOW - 1
    return pl.pallas_call(
        functools.partial(_window_kernel, seq_len=s),
        grid=(b, s // step),
        in_specs=[
            pl.BlockSpec((1, step // TQ, WG_Q, TQ), lambda bi, c: (bi, c, 0, 0)),
            pl.BlockSpec((1, WINDOW, WG_K), lambda bi, c: (bi, jnp.maximum(c * per - 1, 0), 0)),
            pl.BlockSpec((1, step, WG_K), lambda bi, c: (bi, c, 0)),
            pl.BlockSpec((1, WINDOW, WG_K), lambda bi, c: (bi, jnp.minimum(c * per + per, nblk - 1), 0)),
            pl.BlockSpec((1, 1, WG_V, WINDOW), lambda bi, c: (bi, jnp.maximum(c * WIN_TILES - 1, 0), 0, last)),
            pl.BlockSpec((1, WIN_TILES, WG_V, TM), lambda bi, c: (bi, c, 0, 0)),
            pl.BlockSpec((1, 1, WG_V, WINDOW),
                         lambda bi, c: (bi, jnp.minimum(c * WIN_TILES + WIN_TILES, ns - 1), 0, 0)),
            pl.BlockSpec((1, WG_HEADS * WINDOW), lambda bi, c: (0, 0)),
        ],
        out_specs=pl.BlockSpec((1, step, WG_Q), lambda bi, c: (bi, c, 0)),
        out_shape=jax.ShapeDtypeStruct((b, s, WG_Q), BF16),
        compiler_params=_cparams(("arbitrary", "arbitrary")),
        name="window_gqa",
    )(qw, kw, kw, kw, vw, vw, vw, sink_row)


def _out_ffn_kernel(x_ref, oa_ref, ow_ref, mod_ref, gpre_ref, gpost_ref, wa_ref, ww_ref,
                    wg_ref, wu_ref, wd_ref, o_ref, a_ref, h_ref):
    n_sub = x_ref.shape[1] // SUB_ROWS

    def head(r):
        rows = _sub_rows(r)
        y = (jnp.dot(oa_ref[0, rows], wa_ref[...], preferred_element_type=F32)
             + jnp.dot(ow_ref[0, rows], ww_ref[...], preferred_element_type=F32))
        x = x_ref[0, rows] + _post(y, mod_ref, gpost_ref, 1)
        o_ref[0, rows] = x
        h_ref[_slot(r)] = _pre(x, mod_ref, gpre_ref, 2).astype(BF16)

    def tail(r, y):
        rows = _sub_rows(r)
        o_ref[0, rows] = o_ref[0, rows] + _post(y, mod_ref, gpost_ref, 2, coef=0.5)

    head(0)
    _gate_up(h_ref, a_ref, wg_ref, wu_ref, _slot(0))
    for r in range(1, n_sub):
        head(r)
        y = _down(a_ref, wd_ref, _slot(r - 1))
        _gate_up(h_ref, a_ref, wg_ref, wu_ref, _slot(r))
        tail(r - 1, y)
    tail(n_sub - 1, _down(a_ref, wd_ref, _slot(n_sub - 1)))


def _out_ffn_call(x, oa, ow, mod, gpre, gpost, wa, ww, wg, wu, wd):
    b, s, _ = x.shape
    tiles = OUT_FFN_TILES
    return pl.pallas_call(
        _out_ffn_kernel,
        grid=(b, s // (tiles * TM)),
        in_specs=[_row_spec(D_MODEL, tiles), _row_spec(DA_V, tiles), _row_spec(WG_Q, tiles)] + _norm_specs()
        + [_resident((DA_V, D_MODEL)), _resident((WG_Q, D_MODEL))] + _ffn_specs(),
        out_specs=_row_spec(D_MODEL, tiles),
        out_shape=jax.ShapeDtypeStruct(x.shape, F32),
        scratch_shapes=[pltpu.VMEM((TM, D_FF), BF16), pltpu.VMEM((TM, D_MODEL), BF16)],
        compiler_params=_cparams(("arbitrary", "arbitrary")),
        name="out_proj_ffn",
    )(x, oa, ow, mod, gpre, gpost, wa, ww, wg, wu, wd)


def _rope_tables(seq_len):
    pos = jnp.arange(seq_len, dtype=F32)
    inv_freq = 1.0 / (ROPE_THETA ** (jnp.arange(0, HEAD_DIM, 2, dtype=F32) / HEAD_DIM))
    ang = pos[:, None] * inv_freq[None, :]
    cos = jnp.cos(ang)
    sin = jnp.sin(ang)
    cos_h = jnp.concatenate([cos, cos], axis=-1)
    sin_h = jnp.concatenate([-sin, sin], axis=-1)
    cos_k = jnp.concatenate([cos_h, cos_h], axis=-1)
    sin_k = jnp.concatenate([sin_h, sin_h], axis=-1)
    return cos_h.T, sin_h.T, cos_k, sin_k


def kernel(x_prompt, x_sample, c_prompt, c_sample, w_mod, b_mod, norm_pre, norm_post, w_ff_gate, w_ff_up,
           w_ff_down, w_in, w_out, lambda_q1, lambda_k1, lambda_q2, lambda_k2, sink):
    n_prompt = x_prompt.shape[0]
    c_all = jnp.concatenate([c_prompt, c_sample], axis=0)
    n_seq = c_all.shape[0]
    c_all = jnp.pad(c_all, ((0, -n_seq % SUBLANES), (0, 0)))
    mod_all = _mod_call(c_all, w_mod, b_mod)[:, :n_seq]
    mod_all = mod_all.reshape(DEPTH, n_seq, 3 * N_SUB, D_MODEL)

    wg = w_ff_gate.astype(BF16)
    wu = w_ff_up.astype(BF16)
    wd = w_ff_down.astype(BF16)
    o_qa, o_ka, o_va, o_qw, o_kw, o_vw = 0, DA_Q, DA_Q + DA_K, DA_Q + DA_K + DA_V, \
        DA_Q + DA_K + DA_V + WG_Q, DA_Q + DA_K + DA_V + WG_Q + WG_K
    w_in_b = w_in.astype(BF16)
    wt = jnp.concatenate([w_in_b[:, :, o_qa:o_ka], w_in_b[:, :, o_va:o_qw],
                          w_in_b[:, :, o_qw:o_kw], w_in_b[:, :, o_vw:]], axis=-1).transpose(0, 2, 1)
    wk = jnp.concatenate([w_in_b[:, :, o_ka:o_va], w_in_b[:, :, o_kw:o_vw]], axis=-1)
    w_out_b = w_out.astype(BF16)
    sink_rows = jnp.repeat(sink, WINDOW, axis=-1)

    groups = [(x_prompt, slice(0, n_prompt)), (x_sample, slice(n_prompt, None))]
    tables = {x.shape[1]: _rope_tables(x.shape[1]) for x, _ in groups}
    outs = []
    for x, rows in groups:
        tab = tables[x.shape[1]]
        for l in range(DEPTH):
            mod = mod_all[l, rows]
            lambda_init = 0.8 - 0.6 * math.exp(-0.3 * l)
            x, qa, va, qw, vw, ka, kw = _ffn_proj_call(x, mod, norm_pre[l], norm_post[l], wg[l, 0], wu[l, 0],
                                                       wd[l, 0], wt[l], wk[l], *tab)
            oa = _diff_call(qa, ka, va, lambda_q1[l:l + 1], lambda_k1[l:l + 1], lambda_q2[l:l + 1],
                            lambda_k2[l:l + 1], lambda_init)
            ow = _window_call(qw, kw, vw, sink_rows[l:l + 1])
            x = _out_ffn_call(x, oa, ow, mod, norm_pre[l], norm_post[l], w_out_b[l, :DA_V], w_out_b[l, DA_V:],
                              wg[l, 1], wu[l, 1], wd[l, 1])
        outs.append(x)
    return tuple(outs)
```

```python
import functools
import math

import jax
import jax.numpy as jnp
from jax import lax
from jax.experimental import pallas as pl
from jax.experimental.pallas import tpu as pltpu

F32 = jnp.float32
BF16 = jnp.bfloat16

D_MODEL = 1024
DEPTH = 4
HEAD_DIM = 64
HALF = HEAD_DIM // 2
WINDOW = 128
ROPE_THETA = 10000.0
DA_HEADS = 4
DA_VDIM = 2 * HEAD_DIM
WG_HEADS = 8
WG_KV = 2
WG_GROUP = WG_HEADS // WG_KV
DA_Q = DA_HEADS * 2 * HEAD_DIM
DA_K = DA_Q
DA_V = DA_HEADS * DA_VDIM
WG_Q = WG_HEADS * HEAD_DIM
WG_K = WG_KV * HEAD_DIM
WG_V = WG_KV * HEAD_DIM
MIX_WIDTH = DA_V + WG_Q
D_FF = 2816
N_SUB = 3
NORM_EPS = 1e-6
SUBLN_EPS = 1e-5
LOG2E = math.log2(math.e)
Q_SCALE = HEAD_DIM ** -0.5 * LOG2E
BOUND_SLACK = 1.0 + 2.0 ** -12
MIN_COLUMN_SUM = 2.0 ** -80

LANES = 128
SUBLANES = 8
TM = 512
TQ = 256
FF_CHUNK = 256
MOD_TN = 2304
VMEM_LIMIT = 56 * 1024 * 1024


def _cparams(sem):
    return pltpu.CompilerParams(dimension_semantics=sem, vmem_limit_bytes=VMEM_LIMIT)


def _resident(shape):
    return pl.BlockSpec(shape, lambda *_: (0, 0), pipeline_mode=pl.Buffered(1))


def _mod_kernel(c_ref, w_ref, b_ref, o_ref):
    c = c_ref[...]
    a = c * jax.nn.sigmoid(c)
    o_ref[0] = jnp.dot(a, w_ref[0], preferred_element_type=F32,
                       precision=lax.Precision.HIGHEST) + b_ref[0]


def _mod_call(c_all, w_mod, b_mod):
    nb = c_all.shape[0]
    n_out = w_mod.shape[-1]
    return pl.pallas_call(
        _mod_kernel,
        grid=(DEPTH, n_out // MOD_TN),
        in_specs=[
            pl.BlockSpec((nb, D_MODEL), lambda l, n: (0, 0)),
            pl.BlockSpec((1, D_MODEL, MOD_TN), lambda l, n: (l, 0, n)),
            pl.BlockSpec((1, 1, MOD_TN), lambda l, n: (l, 0, n)),
        ],
        out_specs=pl.BlockSpec((1, nb, MOD_TN), lambda l, n: (l, 0, n)),
        out_shape=jax.ShapeDtypeStruct((DEPTH, nb, n_out), F32),
        compiler_params=_cparams(("arbitrary", "arbitrary")),
        name="modulation",
    )(c_all, w_mod, b_mod.reshape(DEPTH, 1, n_out))


def _rms(x, eps):
    return x * lax.rsqrt(jnp.mean(x * x, axis=-1, keepdims=True) + eps)


def _pre(x, mod_ref, gpre_ref, j):
    shift = mod_ref[0, 3 * j:3 * j + 1, :]
    scale = mod_ref[0, 3 * j + 1:3 * j + 2, :]
    return _rms(x, NORM_EPS) * (gpre_ref[j:j + 1, :] * (1.0 + scale)) + shift


def _post(y, mod_ref, gpost_ref, j, coef=1.0):
    gate = mod_ref[0, 3 * j + 2:3 * j + 3, :]
    return _rms(y, NORM_EPS) * (coef * gate * gpost_ref[j:j + 1, :])


SUB_ROWS = TM // 2
HALF_ROWS = [pl.ds(r * SUB_ROWS, SUB_ROWS) for r in range(2)]
OUT_FFN_TILES = 2
FFN_PROJ_TILES = 2
WIN_TILES = 2


def _sub_rows(r):
    return pl.ds(r * SUB_ROWS, SUB_ROWS)


def _slot(r):
    return HALF_ROWS[r % 2]


def _gate_up(h_ref, a_ref, wg_ref, wu_ref, rows):
    for c in range(D_FF // FF_CHUNK):
        sl = slice(c * FF_CHUNK, (c + 1) * FF_CHUNK)
        g = jnp.dot(h_ref[rows], wg_ref[:, sl], preferred_element_type=F32)
        u = jnp.dot(h_ref[rows], wu_ref[:, sl], preferred_element_type=F32)
        a_ref[rows, sl] = (g * jax.nn.sigmoid(g) * u).astype(BF16)


def _down(a_ref, wd_ref, rows):
    return jnp.dot(a_ref[rows], wd_ref[...], preferred_element_type=F32)


def _row_spec(width, tiles=1):
    return pl.BlockSpec((1, tiles * TM, width), lambda bi, si: (bi, si, 0))


def _norm_specs():
    const = lambda bi, si: (0, 0)
    return [pl.BlockSpec((1, 3 * N_SUB, D_MODEL), lambda bi, si: (bi, 0, 0)),
            pl.BlockSpec((N_SUB, D_MODEL), const), pl.BlockSpec((N_SUB, D_MODEL), const)]


def _ffn_specs():
    return [_resident((D_MODEL, D_FF)), _resident((D_MODEL, D_FF)), _resident((D_FF, D_MODEL))]


def _rope_rows(y, cos_t, sin_t, n_groups):
    out = []
    for g in range(n_groups):
        blk = y[g * HEAD_DIM:(g + 1) * HEAD_DIM]
        swapped = jnp.concatenate([blk[HALF:], blk[:HALF]], axis=0)
        out.append((blk * cos_t + swapped * sin_t) * Q_SCALE)
    return jnp.concatenate(out, axis=0)


def _ffn_proj_kernel(x_ref, mod_ref, gpre_ref, gpost_ref, wg_ref, wu_ref, wd_ref, wt_ref, wk_ref,
                     cos_t_ref, sin_t_ref, cos_k_ref, sin_k_ref,
                     xo_ref, qa_ref, va_ref, qw_ref, vw_ref, ka_ref, kw_ref, a_ref, h_ref, hmix_ref):
    hm = SUB_ROWS
    n_sub = x_ref.shape[1] // SUB_ROWS
    lane = lax.broadcasted_iota(jnp.int32, (hm, LANES), 1)
    first_half = (lane % HEAD_DIM) < HALF

    def head(r):
        h_ref[_slot(r)] = _pre(x_ref[0, _sub_rows(r)], mod_ref, gpre_ref, 0).astype(BF16)

    def mid(r, y):
        rows = _sub_rows(r)
        x = x_ref[0, rows] + _post(y, mod_ref, gpost_ref, 0, coef=0.5)
        xo_ref[0, rows] = x
        hmix_ref[_slot(r)] = _pre(x, mod_ref, gpre_ref, 1).astype(BF16)

    def project(r):
        rows = _sub_rows(r)
        tile, half = divmod(r, TM // SUB_ROWS)
        cols = slice(half * hm, (half + 1) * hm)
        h = hmix_ref[_slot(r)]

        k = jnp.dot(h, wk_ref[...], preferred_element_type=F32)
        cos_k = cos_k_ref[rows, :]
        sin_k = sin_k_ref[rows, :]
        for c in range((DA_K + WG_K) // LANES):
            blk = k[:, c * LANES:(c + 1) * LANES]
            swapped = jnp.where(first_half, pltpu.roll(blk, LANES - HALF, 1), pltpu.roll(blk, HALF, 1))
            kr = (blk * cos_k + swapped * sin_k).astype(BF16)
            if c < DA_K // LANES:
                ka_ref[0, rows, c * LANES:(c + 1) * LANES] = kr
            else:
                kw_ref[0, rows, :] = kr

        def t_rows(lo, hi):
            return lax.dot_general(wt_ref[lo:hi, :], h, (((1,), (1,)), ((), ())),
                                   preferred_element_type=F32)

        tok = slice(r * hm, (r + 1) * hm)
        cos_t = cos_t_ref[:, tok]
        sin_t = sin_t_ref[:, tok]
        qa_ref[0, r] = _rope_rows(t_rows(0, DA_Q), cos_t, sin_t, DA_Q // HEAD_DIM).astype(BF16)
        qw_ref[0, r] = _rope_rows(t_rows(DA_Q + DA_V, DA_Q + DA_V + WG_Q), cos_t, sin_t,
                                  WG_Q // HEAD_DIM).astype(BF16)
        va_ref[0, tile, :, cols] = t_rows(DA_Q, DA_Q + DA_V).astype(BF16)
        vw_ref[0, tile, :, cols] = t_rows(DA_Q + DA_V + WG_Q, DA_Q + DA_V + WG_Q + WG_V).astype(BF16)

    head(0)
    _gate_up(h_ref, a_ref, wg_ref, wu_ref, _slot(0))
    for r in range(1, n_sub):
        head(r)
        y = _down(a_ref, wd_ref, _slot(r - 1))
        if r >= 2:
            project(r - 2)
        _gate_up(h_ref, a_ref, wg_ref, wu_ref, _slot(r))
        mid(r - 1, y)
    y = _down(a_ref, wd_ref, _slot(n_sub - 1))
    project(n_sub - 2)
    mid(n_sub - 1, y)
    project(n_sub - 1)


def _ffn_proj_call(x, mod, gpre, gpost, wg, wu, wd, wt, wk, cos_t, sin_t, cos_k, sin_k):
    b, s, _ = x.shape
    ns = s // TM
    tiles = FFN_PROJ_TILES
    return pl.pallas_call(
        _ffn_proj_kernel,
        grid=(b, ns // tiles),
        in_specs=[_row_spec(D_MODEL, tiles)] + _norm_specs() + _ffn_specs() + [
            _resident(wt.shape),
            _resident(wk.shape),
            pl.BlockSpec((HEAD_DIM, tiles * TM), lambda bi, si: (0, si)),
            pl.BlockSpec((HEAD_DIM, tiles * TM), lambda bi, si: (0, si)),
            pl.BlockSpec((tiles * TM, LANES), lambda bi, si: (si, 0)),
            pl.BlockSpec((tiles * TM, LANES), lambda bi, si: (si, 0)),
        ],
        out_specs=[
            _row_spec(D_MODEL, tiles),
            pl.BlockSpec((1, tiles * TM // TQ, DA_Q, TQ), lambda bi, si: (bi, si, 0, 0)),
            pl.BlockSpec((1, tiles, DA_V, TM), lambda bi, si: (bi, si, 0, 0)),
            pl.BlockSpec((1, tiles * TM // TQ, WG_Q, TQ), lambda bi, si: (bi, si, 0, 0)),
            pl.BlockSpec((1, tiles, WG_V, TM), lambda bi, si: (bi, si, 0, 0)),
            _row_spec(DA_K, tiles),
            _row_spec(WG_K, tiles),
        ],
        out_shape=[
            jax.ShapeDtypeStruct(x.shape, F32),
            jax.ShapeDtypeStruct((b, s // TQ, DA_Q, TQ), BF16),
            jax.ShapeDtypeStruct((b, ns, DA_V, TM), BF16),
            jax.ShapeDtypeStruct((b, s // TQ, WG_Q, TQ), BF16),
            jax.ShapeDtypeStruct((b, ns, WG_V, TM), BF16),
            jax.ShapeDtypeStruct((b, s, DA_K), BF16),
            jax.ShapeDtypeStruct((b, s, WG_K), BF16),
        ],
        scratch_shapes=[pltpu.VMEM((TM, D_FF), BF16), pltpu.VMEM((TM, D_MODEL), BF16),
                        pltpu.VMEM((TM, D_MODEL), BF16)],
        compiler_params=_cparams(("arbitrary", "arbitrary")),
        name="ffn_in_proj",
    )(x, mod, gpre, gpost, wg, wu, wd, wt, wk, cos_t, sin_t, cos_k, sin_k)


def _diff_kernel(q_ref, k_ref, v_ref, lq1_ref, lk1_ref, lq2_ref, lk2_ref, o_ref, knorm_ref, *,
                 lambda_init, n_chunks, n_qblocks, unroll_c, unroll_q):
    lam = (jnp.exp(jnp.sum(lq1_ref[...] * lk1_ref[...], axis=-1, keepdims=True))
           - jnp.exp(jnp.sum(lq2_ref[...] * lk2_ref[...], axis=-1, keepdims=True)) + lambda_init)
    row = lax.broadcasted_iota(jnp.int32, (2 * HEAD_DIM, TQ), 0)

    @pl.when(pl.program_id(2) == 0)
    def _():
        def body(j, mx):
            kc = k_ref[0, pl.ds(pl.multiple_of(j * TM, TM), TM), :].astype(F32)
            return jnp.maximum(mx, jnp.sum(kc * kc, axis=-1, keepdims=True))
        mx = lax.fori_loop(0, n_chunks, body, jnp.zeros((TM, 1), F32), unroll=min(n_chunks, 4))
        knorm_ref[...] = jnp.broadcast_to(jnp.sqrt(jnp.max(mx, axis=0, keepdims=True)), knorm_ref.shape)

    def padded_q(qb):
        q = q_ref[0, qb]
        zero = jnp.zeros_like(q)
        return jnp.concatenate([jnp.where(row < HEAD_DIM, q, zero),
                                jnp.where(row >= HEAD_DIM, q, zero)], axis=1)

    def p_times_v(j, pb):
        vt = v_ref[0, j]
        return jnp.concatenate(
            [jnp.dot(vt, pb[:, :TQ], preferred_element_type=F32),
             jnp.dot(vt, pb[:, TQ:], preferred_element_type=F32)], axis=1)

    def finish(qb, l, acc):
        o = acc / l
        o = o[:, :TQ] - lam * o[:, TQ:]
        o = o * lax.rsqrt(jnp.mean(o * o, axis=0, keepdims=True) + SUBLN_EPS) * (1.0 - lambda_init)
        o_ref[0, pl.ds(pl.multiple_of(qb * TQ, TQ), TQ), :] = o.T.astype(BF16)

    def key_chunk(j):
        return k_ref[0, pl.ds(pl.multiple_of(j * TM, TM), TM), :]

    kmax = knorm_ref[0:1, :]
    kmax = jnp.concatenate([kmax] * (2 * TQ // LANES), axis=1)

    def q_block_bounded(qb, carry):
        l_min, l_prev, acc_prev = carry
        qpad = padded_q(qb)
        qf = qpad.astype(F32)
        shift = jnp.sqrt(jnp.sum(qf * qf, axis=0, keepdims=True)) * kmax * BOUND_SLACK

        def logits(j):
            return jnp.dot(key_chunk(j), qpad, preferred_element_type=F32)

        def run(j0, st, after_first_logits=None):
            l8, acc = st
            s = logits(j0)
            if after_first_logits is not None:
                after_first_logits()
            for i in range(unroll_c):
                s_next = logits(j0 + i + 1) if i + 1 < unroll_c else None
                p = jnp.exp2(s - shift)
                l8 = l8 + jnp.sum(p.reshape(TM // SUBLANES, SUBLANES, 2 * TQ), axis=0)
                acc = acc + p_times_v(j0 + i, p.astype(BF16))
                s = s_next
            return l8, acc

        def finish_previous():
            finish(jnp.maximum(qb - 1, 0), l_prev, acc_prev)

        st = (jnp.zeros((SUBLANES, 2 * TQ), F32), jnp.zeros((DA_VDIM, 2 * TQ), F32))
        st = run(0, st, finish_previous)
        if n_chunks > unroll_c:
            st = lax.fori_loop(1, n_chunks // unroll_c, lambda g, st: run(g * unroll_c, st), st)
        l8, acc = st
        l = jnp.sum(l8, axis=0, keepdims=True)
        return jnp.minimum(l_min, l), l, acc

    def q_block_online(qb, carry):
        qpad = padded_q(qb)

        def chunk(j, st):
            m, l, acc = st
            s = jnp.dot(key_chunk(j), qpad, preferred_element_type=F32)
            m_new = jnp.maximum(m, jnp.max(s, axis=0, keepdims=True))
            alpha = jnp.exp2(m - m_new)
            p = jnp.exp2(s - m_new)
            l = alpha * l + jnp.sum(p, axis=0, keepdims=True)
            return m_new, l, acc * alpha + p_times_v(j, p.astype(BF16))

        init = (jnp.full((1, 2 * TQ), -jnp.inf, F32), jnp.zeros((1, 2 * TQ), F32),
                jnp.zeros((DA_VDIM, 2 * TQ), F32))
        _, l, acc = lax.fori_loop(0, n_chunks, chunk, init)
        finish(qb, l, acc)
        return carry

    init = (jnp.full((1, 2 * TQ), jnp.inf, F32), jnp.ones((1, 2 * TQ), F32), jnp.zeros((DA_VDIM, 2 * TQ), F32))
    l_min, l_last, acc_last = lax.fori_loop(0, n_qblocks, q_block_bounded, init, unroll=unroll_q)
    finish(n_qblocks - 1, l_last, acc_last)
    bounded_ok = jnp.min(l_min) >= MIN_COLUMN_SUM

    @pl.when(jnp.logical_not(bounded_ok))
    def _():
        lax.fori_loop(0, n_qblocks, q_block_online, 0)


def _diff_call(qa, ka, va, lq1, lk1, lq2, lk2, lambda_init):
    b, s, _ = ka.shape
    tq_outer = min(s, 2048)
    n_qblocks = tq_outer // TQ
    n_chunks = s // TM
    vec = pl.BlockSpec((1, HEAD_DIM), lambda bi, h, qi: (0, 0))
    return pl.pallas_call(
        functools.partial(_diff_kernel, lambda_init=lambda_init, n_chunks=n_chunks, n_qblocks=n_qblocks,
                          unroll_c=min(n_chunks, 32), unroll_q=n_qblocks if n_chunks <= 4 else 1),
        grid=(b, DA_HEADS, s // tq_outer),
        in_specs=[
            pl.BlockSpec((1, n_qblocks, 2 * HEAD_DIM, TQ), lambda bi, h, qi: (bi, qi, h, 0)),
            pl.BlockSpec((1, s, 2 * HEAD_DIM), lambda bi, h, qi: (bi, 0, h)),
            pl.BlockSpec((1, n_chunks, DA_VDIM, TM), lambda bi, h, qi: (bi, 0, h, 0)),
            vec, vec, vec, vec,
        ],
        out_specs=pl.BlockSpec((1, tq_outer, DA_VDIM), lambda bi, h, qi: (bi, qi, h)),
        out_shape=jax.ShapeDtypeStruct((b, s, DA_V), BF16),
        scratch_shapes=[pltpu.VMEM((SUBLANES, LANES), F32)],
        compiler_params=_cparams(("arbitrary", "arbitrary", "arbitrary")),
        name="diff_attention",
    )(qa, ka, va, lq1, lk1, lq2, lk2)


def _window_kernel(q_ref, kp_ref, kc_ref, kn_ref, vp_ref, vc_ref, vn_ref, sink_ref, o_ref, *, seq_len):
    c = pl.program_id(1)
    step = kc_ref.shape[1]
    keys = jnp.concatenate([kp_ref[0], kc_ref[0], kn_ref[0]], axis=0)
    vals = jnp.concatenate([vp_ref[0, 0]] + [vc_ref[0, t] for t in range(vc_ref.shape[1])]
                           + [vn_ref[0, 0]], axis=1)
    sink = sink_ref[...] * LOG2E
    n_win = 3 * WINDOW
    kidx = lax.broadcasted_iota(jnp.int32, (n_win, WINDOW), 0)
    qidx = lax.broadcasted_iota(jnp.int32, (n_win, WINDOW), 1)
    band = jnp.abs(kidx - WINDOW - qidx) <= WINDOW
    zero_q = jnp.zeros((HEAD_DIM, WINDOW), BF16)

    ksq = keys.astype(F32)
    ksq = ksq * ksq
    lane = lax.broadcasted_iota(jnp.int32, ksq.shape, 1)
    kmax = []
    for g in range(WG_KV):
        mine = (lane >= g * HEAD_DIM) & (lane < (g + 1) * HEAD_DIM)
        n2 = jnp.sum(jnp.where(mine, ksq, 0.0), axis=-1, keepdims=True)
        kmax.append(jnp.sqrt(jnp.max(n2, axis=0, keepdims=True)))

    def attend(bounded):
        def logits(qs, g):
            blocks = []
            for hh in range(WG_GROUP):
                hd = g * WG_GROUP + hh
                qh = q_ref[0, qs // 2, hd * HEAD_DIM:(hd + 1) * HEAD_DIM,
                           (qs % 2) * WINDOW:(qs % 2 + 1) * WINDOW]
                pad = [qh, zero_q] if g == 0 else [zero_q, qh]
                blocks.append(jnp.concatenate(pad, axis=0))
            qpad = jnp.concatenate(blocks, axis=1)
            kwin = keys[qs * WINDOW:qs * WINDOW + n_win]
            return qpad, jnp.dot(kwin, qpad, preferred_element_type=F32)

        tiles = [(qs, g) for qs in range(step // WINDOW) for g in range(WG_KV)]
        l_min = jnp.full((1, WG_GROUP * WINDOW), jnp.inf, F32)
        outs = []
        nxt = logits(*tiles[0])
        for t, (qs, g) in enumerate(tiles):
            qpad, s = nxt
            if t + 1 < len(tiles):
                nxt = logits(*tiles[t + 1])
            kpos = c * step + (qs - 1) * WINDOW + kidx
            valid1 = band & (kpos >= 0) & (kpos < seq_len)
            bias1 = jnp.where(valid1, 0.0, -jnp.inf).astype(F32)
            s = s + jnp.concatenate([bias1] * WG_GROUP, axis=1)
            vwin = vals[:, qs * WINDOW:qs * WINDOW + n_win]
            sk = sink[:, g * WG_GROUP * WINDOW:(g + 1) * WG_GROUP * WINDOW]
            if bounded:
                qf = qpad.astype(F32)
                top = jnp.sqrt(jnp.sum(qf * qf, axis=0, keepdims=True)) * kmax[g] * BOUND_SLACK
            else:
                top = jnp.max(s, axis=0, keepdims=True)
            shift = jnp.maximum(top, sk)
            p = jnp.exp2(s - shift)
            l = jnp.sum(p, axis=0, keepdims=True) + jnp.exp2(sk - shift)
            l_min = jnp.minimum(l_min, l)
            pv = jnp.dot(vwin, p.astype(BF16), preferred_element_type=F32)
            o = pv[g * HEAD_DIM:(g + 1) * HEAD_DIM] / l
            for hh in range(WG_GROUP):
                outs.append(o[:, hh * WINDOW:(hh + 1) * WINDOW])
            if g == WG_KV - 1:
                o_all = jnp.concatenate(outs, axis=0)
                o_ref[0, qs * WINDOW:(qs + 1) * WINDOW, :] = o_all.T.astype(BF16)
                outs = []
        return l_min

    bounded_ok = jnp.min(attend(True)) >= MIN_COLUMN_SUM

    @pl.when(jnp.logical_not(bounded_ok))
    def _():
        attend(False)


def _window_call(qw, kw, vw, sink_row):
    b, s, _ = kw.shape
    ns = s // TM
    nblk = s // WINDOW
    step = WIN_TILES * TM
    per = step // WINDOW
    last = TM // WINDOW - 1
    return pl.pallas_call(
        functools.partial(_window_kernel, seq_len=s),
        grid=(b, s // step),
        in_specs=[
            pl.BlockSpec((1, step // TQ, WG_Q, TQ), lambda bi, c: (bi, c, 0, 0)),
            pl.BlockSpec((1, WINDOW, WG_K), lambda bi, c: (bi, jnp.maximum(c * per - 1, 0), 0)),
            pl.BlockSpec((1, step, WG_K), lambda bi, c: (bi, c, 0)),
            pl.BlockSpec((1, WINDOW, WG_K), lambda bi, c: (bi, jnp.minimum(c * per + per, nblk - 1), 0)),
            pl.BlockSpec((1, 1, WG_V, WINDOW), lambda bi, c: (bi, jnp.maximum(c * WIN_TILES - 1, 0), 0, last)),
            pl.BlockSpec((1, WIN_TILES, WG_V, TM), lambda bi, c: (bi, c, 0, 0)),
            pl.BlockSpec((1, 1, WG_V, WINDOW),
                         lambda bi, c: (bi, jnp.minimum(c * WIN_TILES + WIN_TILES, ns - 1), 0, 0)),
            pl.BlockSpec((1, WG_HEADS * WINDOW), lambda bi, c: (0, 0)),
        ],
        out_specs=pl.BlockSpec((1, step, WG_Q), lambda bi, c: (bi, c, 0)),
        out_shape=jax.ShapeDtypeStruct((b, s, WG_Q), BF16),
        compiler_params=_cparams(("arbitrary", "arbitrary")),
        name="window_gqa",
    )(qw, kw, kw, kw, vw, vw, vw, sink_row)


def _out_ffn_kernel(x_ref, oa_ref, ow_ref, mod_ref, gpre_ref, gpost_ref, wa_ref, ww_ref,
                    wg_ref, wu_ref, wd_ref, o_ref, a_ref, h_ref):
    n_sub = x_ref.shape[1] // SUB_ROWS

    def head(r):
        rows = _sub_rows(r)
        y = (jnp.dot(oa_ref[0, rows], wa_ref[...], preferred_element_type=F32)
             + jnp.dot(ow_ref[0, rows], ww_ref[...], preferred_element_type=F32))
        x = x_ref[0, rows] + _post(y, mod_ref, gpost_ref, 1)
        o_ref[0, rows] = x
        h_ref[_slot(r)] = _pre(x, mod_ref, gpre_ref, 2).astype(BF16)

    def tail(r, y):
        rows = _sub_rows(r)
        o_ref[0, rows] = o_ref[0, rows] + _post(y, mod_ref, gpost_ref, 2, coef=0.5)

    head(0)
    _gate_up(h_ref, a_ref, wg_ref, wu_ref, _slot(0))
    for r in range(1, n_sub):
        head(r)
        y = _down(a_ref, wd_ref, _slot(r - 1))
        _gate_up(h_ref, a_ref, wg_ref, wu_ref, _slot(r))
        tail(r - 1, y)
    tail(n_sub - 1, _down(a_ref, wd_ref, _slot(n_sub - 1)))


def _out_ffn_call(x, oa, ow, mod, gpre, gpost, wa, ww, wg, wu, wd):
    b, s, _ = x.shape
    tiles = OUT_FFN_TILES
    return pl.pallas_call(
        _out_ffn_kernel,
        grid=(b, s // (tiles * TM)),
        in_specs=[_row_spec(D_MODEL, tiles), _row_spec(DA_V, tiles), _row_spec(WG_Q, tiles)] + _norm_specs()
        + [_resident((DA_V, D_MODEL)), _resident((WG_Q, D_MODEL))] + _ffn_specs(),
        out_specs=_row_spec(D_MODEL, tiles),
        out_shape=jax.ShapeDtypeStruct(x.shape, F32),
        scratch_shapes=[pltpu.VMEM((TM, D_FF), BF16), pltpu.VMEM((TM, D_MODEL), BF16)],
        compiler_params=_cparams(("arbitrary", "arbitrary")),
        name="out_proj_ffn",
    )(x, oa, ow, mod, gpre, gpost, wa, ww, wg, wu, wd)


def _rope_tables(seq_len):
    pos = jnp.arange(seq_len, dtype=F32)
    inv_freq = 1.0 / (ROPE_THETA ** (jnp.arange(0, HEAD_DIM, 2, dtype=F32) / HEAD_DIM))
    ang = pos[:, None] * inv_freq[None, :]
    cos = jnp.cos(ang)
    sin = jnp.sin(ang)
    cos_h = jnp.concatenate([cos, cos], axis=-1)
    sin_h = jnp.concatenate([-sin, sin], axis=-1)
    cos_k = jnp.concatenate([cos_h, cos_h], axis=-1)
    sin_k = jnp.concatenate([sin_h, sin_h], axis=-1)
    return cos_h.T, sin_h.T, cos_k, sin_k


def kernel(x_prompt, x_sample, c_prompt, c_sample, w_mod, b_mod, norm_pre, norm_post, w_ff_gate, w_ff_up,
           w_ff_down, w_in, w_out, lambda_q1, lambda_k1, lambda_q2, lambda_k2, sink):
    n_prompt = x_prompt.shape[0]
    c_all = jnp.concatenate([c_prompt, c_sample], axis=0)
    n_seq = c_all.shape[0]
    c_all = jnp.pad(c_all, ((0, -n_seq % SUBLANES), (0, 0)))
    mod_all = _mod_call(c_all, w_mod, b_mod)[:, :n_seq]
    mod_all = mod_all.reshape(DEPTH, n_seq, 3 * N_SUB, D_MODEL)

    wg = w_ff_gate.astype(BF16)
    wu = w_ff_up.astype(BF16)
    wd = w_ff_down.astype(BF16)
    o_qa, o_ka, o_va, o_qw, o_kw, o_vw = 0, DA_Q, DA_Q + DA_K, DA_Q + DA_K + DA_V, \
        DA_Q + DA_K + DA_V + WG_Q, DA_Q + DA_K + DA_V + WG_Q + WG_K
    w_in_b = w_in.astype(BF16)
    wt = jnp.concatenate([w_in_b[:, :, o_qa:o_ka], w_in_b[:, :, o_va:o_qw],
                          w_in_b[:, :, o_qw:o_kw], w_in_b[:, :, o_vw:]], axis=-1).transpose(0, 2, 1)
    wk = jnp.concatenate([w_in_b[:, :, o_ka:o_va], w_in_b[:, :, o_kw:o_vw]], axis=-1)
    w_out_b = w_out.astype(BF16)
    sink_rows = jnp.repeat(sink, WINDOW, axis=-1)

    groups = [(x_prompt, slice(0, n_prompt)), (x_sample, slice(n_prompt, None))]
    cos_t, sin_t, cos_k, sin_k = _rope_tables(max(x.shape[1] for x, _ in groups))
    tables = {x.shape[1]: (cos_t[:, :x.shape[1]], sin_t[:, :x.shape[1]], cos_k[:x.shape[1]], sin_k[:x.shape[1]])
              for x, _ in groups}
    outs = []
    for x, rows in groups:
        tab = tables[x.shape[1]]
        for l in range(DEPTH):
            mod = mod_all[l, rows]
            lambda_init = 0.8 - 0.6 * math.exp(-0.3 * l)
            x, qa, va, qw, vw, ka, kw = _ffn_proj_call(x, mod, norm_pre[l], norm_post[l], wg[l, 0], wu[l, 0],
                                                       wd[l, 0], wt[l], wk[l], *tab)
            oa = _diff_call(qa, ka, va, lambda_q1[l:l + 1], lambda_k1[l:l + 1], lambda_q2[l:l + 1],
                            lambda_k2[l:l + 1], lambda_init)
            ow = _window_call(qw, kw, vw, sink_rows[l:l + 1])
            x = _out_ffn_call(x, oa, ow, mod, norm_pre[l], norm_post[l], w_out_b[l, :DA_V], w_out_b[l, DA_V:],
                              wg[l, 1], wu[l, 1], wd[l, 1])
        outs.append(x)
    return tuple(outs)
```

```python
import functools
import math

import jax
import jax.numpy as jnp
from jax import lax
from jax.experimental import pallas as pl
from jax.experimental.pallas import tpu as pltpu

F32 = jnp.float32
BF16 = jnp.bfloat16

D_MODEL = 1024
DEPTH = 4
HEAD_DIM = 64
HALF = HEAD_DIM // 2
WINDOW = 128
ROPE_THETA = 10000.0
DA_HEADS = 4
DA_VDIM = 2 * HEAD_DIM
WG_HEADS = 8
WG_KV = 2
WG_GROUP = WG_HEADS // WG_KV
DA_Q = DA_HEADS * 2 * HEAD_DIM
DA_K = DA_Q
DA_V = DA_HEADS * DA_VDIM
WG_Q = WG_HEADS * HEAD_DIM
WG_K = WG_KV * HEAD_DIM
WG_V = WG_KV * HEAD_DIM
MIX_WIDTH = DA_V + WG_Q
D_FF = 2816
N_SUB = 3
NORM_EPS = 1e-6
SUBLN_EPS = 1e-5
LOG2E = math.log2(math.e)
Q_SCALE = HEAD_DIM ** -0.5 * LOG2E
BOUND_SLACK = 1.0 + 2.0 ** -12
MIN_COLUMN_SUM = 2.0 ** -80

LANES = 128
SUBLANES = 8
TM = 512
TQ = 256
FF_CHUNK = 256
MOD_TN = 2304
VMEM_LIMIT = 56 * 1024 * 1024


def _cparams(sem):
    return pltpu.CompilerParams(dimension_semantics=sem, vmem_limit_bytes=VMEM_LIMIT)


def _resident(shape):
    return pl.BlockSpec(shape, lambda *_: (0, 0), pipeline_mode=pl.Buffered(1))


def _mod_kernel(c_ref, w_ref, b_ref, o_ref):
    c = c_ref[...]
    a = c * jax.nn.sigmoid(c)
    o_ref[0] = jnp.dot(a, w_ref[0], preferred_element_type=F32,
                       precision=lax.Precision.HIGHEST) + b_ref[0]


def _mod_call(c_all, w_mod, b_mod):
    nb = c_all.shape[0]
    n_out = w_mod.shape[-1]
    return pl.pallas_call(
        _mod_kernel,
        grid=(DEPTH, n_out // MOD_TN),
        in_specs=[
            pl.BlockSpec((nb, D_MODEL), lambda l, n: (0, 0)),
            pl.BlockSpec((1, D_MODEL, MOD_TN), lambda l, n: (l, 0, n)),
            pl.BlockSpec((1, 1, MOD_TN), lambda l, n: (l, 0, n)),
        ],
        out_specs=pl.BlockSpec((1, nb, MOD_TN), lambda l, n: (l, 0, n)),
        out_shape=jax.ShapeDtypeStruct((DEPTH, nb, n_out), F32),
        compiler_params=_cparams(("arbitrary", "arbitrary")),
        name="modulation",
    )(c_all, w_mod, b_mod.reshape(DEPTH, 1, n_out))


def _rms(x, eps):
    return x * lax.rsqrt(jnp.mean(x * x, axis=-1, keepdims=True) + eps)


def _pre(x, mod_ref, gpre_ref, j):
    shift = mod_ref[0, 3 * j:3 * j + 1, :]
    scale = mod_ref[0, 3 * j + 1:3 * j + 2, :]
    return _rms(x, NORM_EPS) * (gpre_ref[j:j + 1, :] * (1.0 + scale)) + shift


def _post(y, mod_ref, gpost_ref, j, coef=1.0):
    gate = mod_ref[0, 3 * j + 2:3 * j + 3, :]
    return _rms(y, NORM_EPS) * (coef * gate * gpost_ref[j:j + 1, :])


SUB_ROWS = TM // 2
HALF_ROWS = [pl.ds(r * SUB_ROWS, SUB_ROWS) for r in range(2)]
OUT_FFN_TILES = 2
FFN_PROJ_TILES = 2
WIN_TILES = 2
DIFF_Q_STEP = 2048
DIFF_RUN_CHUNKS = 32


def _sub_rows(r):
    return pl.ds(r * SUB_ROWS, SUB_ROWS)


def _slot(r):
    return HALF_ROWS[r % 2]


def _gate_up(h_ref, a_ref, wg_ref, wu_ref, rows):
    for c in range(D_FF // FF_CHUNK):
        sl = slice(c * FF_CHUNK, (c + 1) * FF_CHUNK)
        g = jnp.dot(h_ref[rows], wg_ref[:, sl], preferred_element_type=F32)
        u = jnp.dot(h_ref[rows], wu_ref[:, sl], preferred_element_type=F32)
        a_ref[rows, sl] = (g * jax.nn.sigmoid(g) * u).astype(BF16)


def _down(a_ref, wd_ref, rows):
    return jnp.dot(a_ref[rows], wd_ref[...], preferred_element_type=F32)


def _row_spec(width, tiles=1):
    return pl.BlockSpec((1, tiles * TM, width), lambda bi, si: (bi, si, 0))


def _norm_specs():
    const = lambda bi, si: (0, 0)
    return [pl.BlockSpec((1, 3 * N_SUB, D_MODEL), lambda bi, si: (bi, 0, 0)),
            pl.BlockSpec((N_SUB, D_MODEL), const), pl.BlockSpec((N_SUB, D_MODEL), const)]


def _ffn_specs():
    return [_resident((D_MODEL, D_FF)), _resident((D_MODEL, D_FF)), _resident((D_FF, D_MODEL))]


def _rope_rows(y, cos_t, sin_t, n_groups):
    out = []
    for g in range(n_groups):
        blk = y[g * HEAD_DIM:(g + 1) * HEAD_DIM]
        swapped = jnp.concatenate([blk[HALF:], blk[:HALF]], axis=0)
        out.append((blk * cos_t + swapped * sin_t) * Q_SCALE)
    return jnp.concatenate(out, axis=0)


def _ffn_proj_kernel(x_ref, mod_ref, gpre_ref, gpost_ref, wg_ref, wu_ref, wd_ref, wt_ref, wk_ref,
                     cos_t_ref, sin_t_ref, cos_k_ref, sin_k_ref,
                     xo_ref, qa_ref, va_ref, qw_ref, vw_ref, ka_ref, kw_ref, a_ref, h_ref, hmix_ref):
    hm = SUB_ROWS
    n_sub = x_ref.shape[1] // SUB_ROWS
    lane = lax.broadcasted_iota(jnp.int32, (hm, LANES), 1)
    first_half = (lane % HEAD_DIM) < HALF

    def head(r):
        h_ref[_slot(r)] = _pre(x_ref[0, _sub_rows(r)], mod_ref, gpre_ref, 0).astype(BF16)

    def mid(r, y):
        rows = _sub_rows(r)
        x = x_ref[0, rows] + _post(y, mod_ref, gpost_ref, 0, coef=0.5)
        xo_ref[0, rows] = x
        hmix_ref[_slot(r)] = _pre(x, mod_ref, gpre_ref, 1).astype(BF16)

    def project(r):
        rows = _sub_rows(r)
        tile, half = divmod(r, TM // SUB_ROWS)
        cols = slice(half * hm, (half + 1) * hm)
        h = hmix_ref[_slot(r)]

        k = jnp.dot(h, wk_ref[...], preferred_element_type=F32)
        cos_k = cos_k_ref[rows, :]
        sin_k = sin_k_ref[rows, :]
        for c in range((DA_K + WG_K) // LANES):
            blk = k[:, c * LANES:(c + 1) * LANES]
            swapped = jnp.where(first_half, pltpu.roll(blk, LANES - HALF, 1), pltpu.roll(blk, HALF, 1))
            kr = (blk * cos_k + swapped * sin_k).astype(BF16)
            if c < DA_K // LANES:
                ka_ref[0, c, rows, :] = kr
            else:
                kw_ref[0, rows, :] = kr

        def t_rows(lo, hi):
            return lax.dot_general(wt_ref[lo:hi, :], h, (((1,), (1,)), ((), ())),
                                   preferred_element_type=F32)

        tok = slice(r * hm, (r + 1) * hm)
        cos_t = cos_t_ref[:, tok]
        sin_t = sin_t_ref[:, tok]
        qa_ref[0, r] = _rope_rows(t_rows(0, DA_Q), cos_t, sin_t, DA_Q // HEAD_DIM).astype(BF16)
        qw_ref[0, r] = _rope_rows(t_rows(DA_Q + DA_V, DA_Q + DA_V + WG_Q), cos_t, sin_t,
                                  WG_Q // HEAD_DIM).astype(BF16)
        va_ref[0, tile, :, cols] = t_rows(DA_Q, DA_Q + DA_V).astype(BF16)
        vw_ref[0, tile, :, cols] = t_rows(DA_Q + DA_V + WG_Q, DA_Q + DA_V + WG_Q + WG_V).astype(BF16)

    head(0)
    _gate_up(h_ref, a_ref, wg_ref, wu_ref, _slot(0))
    for r in range(1, n_sub):
        head(r)
        y = _down(a_ref, wd_ref, _slot(r - 1))
        if r >= 2:
            project(r - 2)
        _gate_up(h_ref, a_ref, wg_ref, wu_ref, _slot(r))
        mid(r - 1, y)
    y = _down(a_ref, wd_ref, _slot(n_sub - 1))
    project(n_sub - 2)
    mid(n_sub - 1, y)
    project(n_sub - 1)


def _ffn_proj_call(x, mod, gpre, gpost, wg, wu, wd, wt, wk, cos_t, sin_t, cos_k, sin_k):
    b, s, _ = x.shape
    ns = s // TM
    tiles = FFN_PROJ_TILES
    return pl.pallas_call(
        _ffn_proj_kernel,
        grid=(b, ns // tiles),
        in_specs=[_row_spec(D_MODEL, tiles)] + _norm_specs() + _ffn_specs() + [
            _resident(wt.shape),
            _resident(wk.shape),
            pl.BlockSpec((HEAD_DIM, tiles * TM), lambda bi, si: (0, si)),
            pl.BlockSpec((HEAD_DIM, tiles * TM), lambda bi, si: (0, si)),
            pl.BlockSpec((tiles * TM, LANES), lambda bi, si: (si, 0)),
            pl.BlockSpec((tiles * TM, LANES), lambda bi, si: (si, 0)),
        ],
        out_specs=[
            _row_spec(D_MODEL, tiles),
            pl.BlockSpec((1, tiles * TM // TQ, DA_Q, TQ), lambda bi, si: (bi, si, 0, 0)),
            pl.BlockSpec((1, tiles, DA_V, TM), lambda bi, si: (bi, si, 0, 0)),
            pl.BlockSpec((1, tiles * TM // TQ, WG_Q, TQ), lambda bi, si: (bi, si, 0, 0)),
            pl.BlockSpec((1, tiles, WG_V, TM), lambda bi, si: (bi, si, 0, 0)),
            pl.BlockSpec((1, DA_HEADS, tiles * TM, 2 * HEAD_DIM), lambda bi, si: (bi, 0, si, 0)),
            _row_spec(WG_K, tiles),
        ],
        out_shape=[
            jax.ShapeDtypeStruct(x.shape, F32),
            jax.ShapeDtypeStruct((b, s // TQ, DA_Q, TQ), BF16),
            jax.ShapeDtypeStruct((b, ns, DA_V, TM), BF16),
            jax.ShapeDtypeStruct((b, s // TQ, WG_Q, TQ), BF16),
            jax.ShapeDtypeStruct((b, ns, WG_V, TM), BF16),
            jax.ShapeDtypeStruct((b, DA_HEADS, s, 2 * HEAD_DIM), BF16),
            jax.ShapeDtypeStruct((b, s, WG_K), BF16),
        ],
        scratch_shapes=[pltpu.VMEM((TM, D_FF), BF16), pltpu.VMEM((TM, D_MODEL), BF16),
                        pltpu.VMEM((TM, D_MODEL), BF16)],
        compiler_params=_cparams(("arbitrary", "arbitrary")),
        name="ffn_in_proj",
    )(x, mod, gpre, gpost, wg, wu, wd, wt, wk, cos_t, sin_t, cos_k, sin_k)


def _diff_kernel(q_ref, k_ref, v_ref, lq1_ref, lk1_ref, lq2_ref, lk2_ref, o_ref, knorm_ref, *,
                 lambda_init, n_chunks, n_qblocks, unroll_c, unroll_q):
    lam = (jnp.exp(jnp.sum(lq1_ref[...] * lk1_ref[...], axis=-1, keepdims=True))
           - jnp.exp(jnp.sum(lq2_ref[...] * lk2_ref[...], axis=-1, keepdims=True)) + lambda_init)
    row = lax.broadcasted_iota(jnp.int32, (2 * HEAD_DIM, TQ), 0)

    @pl.when(pl.program_id(2) == 0)
    def _():
        def body(j, mx):
            kc = k_ref[0, 0, pl.ds(pl.multiple_of(j * TM, TM), TM), :].astype(F32)
            return jnp.maximum(mx, jnp.sum(kc * kc, axis=-1, keepdims=True))
        mx = lax.fori_loop(0, n_chunks, body, jnp.zeros((TM, 1), F32))
        knorm_ref[...] = jnp.broadcast_to(jnp.sqrt(jnp.max(mx, axis=0, keepdims=True)), knorm_ref.shape)

    def padded_q(qb):
        q = q_ref[0, qb]
        zero = jnp.zeros_like(q)
        return jnp.concatenate([jnp.where(row < HEAD_DIM, q, zero),
                                jnp.where(row >= HEAD_DIM, q, zero)], axis=1)

    def p_times_v(j, pb):
        vt = v_ref[0, j]
        return jnp.concatenate(
            [jnp.dot(vt, pb[:, :TQ], preferred_element_type=F32),
             jnp.dot(vt, pb[:, TQ:], preferred_element_type=F32)], axis=1)

    def finish(qb, l, acc):
        o = acc / l
        o = o[:, :TQ] - lam * o[:, TQ:]
        o = o * lax.rsqrt(jnp.mean(o * o, axis=0, keepdims=True) + SUBLN_EPS) * (1.0 - lambda_init)
        o_ref[0, 0, pl.ds(pl.multiple_of(qb * TQ, TQ), TQ), :] = o.T.astype(BF16)

    def key_chunk(j):
        return k_ref[0, 0, pl.ds(pl.multiple_of(j * TM, TM), TM), :]

    kmax = knorm_ref[0:1, :]
    kmax = jnp.concatenate([kmax] * (2 * TQ // LANES), axis=1)

    def q_block_bounded(qb, l_min):
        qpad = padded_q(qb)
        qf = qpad.astype(F32)
        shift = jnp.sqrt(jnp.sum(qf * qf, axis=0, keepdims=True)) * kmax * BOUND_SLACK

        def logits(j):
            return jnp.dot(key_chunk(j), qpad, preferred_element_type=F32)

        def run(j0, st):
            l8, acc = st
            s = logits(j0)
            for i in range(unroll_c):
                s_next = logits(j0 + i + 1) if i + 1 < unroll_c else None
                p = jnp.exp2(s - shift)
                l8 = l8 + jnp.sum(p.reshape(TM // SUBLANES, SUBLANES, 2 * TQ), axis=0)
                acc = acc + p_times_v(j0 + i, p.astype(BF16))
                s = s_next
            return l8, acc

        st = (jnp.zeros((SUBLANES, 2 * TQ), F32), jnp.zeros((DA_VDIM, 2 * TQ), F32))
        if n_chunks == unroll_c:
            st = run(0, st)
        else:
            st = lax.fori_loop(0, n_chunks // unroll_c, lambda g, st: run(g * unroll_c, st), st)
        l8, acc = st
        l = jnp.sum(l8, axis=0, keepdims=True)
        finish(qb, l, acc)
        return jnp.minimum(l_min, l)

    def q_block_online(qb, carry):
        qpad = padded_q(qb)

        def chunk(j, st):
            m, l, acc = st
            s = jnp.dot(key_chunk(j), qpad, preferred_element_type=F32)
            m_new = jnp.maximum(m, jnp.max(s, axis=0, keepdims=True))
            alpha = jnp.exp2(m - m_new)
            p = jnp.exp2(s - m_new)
            l = alpha * l + jnp.sum(p, axis=0, keepdims=True)
            return m_new, l, acc * alpha + p_times_v(j, p.astype(BF16))

        init = (jnp.full((1, 2 * TQ), -jnp.inf, F32), jnp.zeros((1, 2 * TQ), F32),
                jnp.zeros((DA_VDIM, 2 * TQ), F32))
        _, l, acc = lax.fori_loop(0, n_chunks, chunk, init)
        finish(qb, l, acc)
        return carry

    l_min = lax.fori_loop(0, n_qblocks, q_block_bounded, jnp.full((1, 2 * TQ), jnp.inf, F32),
                          unroll=unroll_q)
    bounded_ok = jnp.min(l_min) >= MIN_COLUMN_SUM

    @pl.when(jnp.logical_not(bounded_ok))
    def _():
        lax.fori_loop(0, n_qblocks, q_block_online, 0)


def _diff_call(qa, ka, va, lq1, lk1, lq2, lk2, lambda_init):
    b, _, s, _ = ka.shape
    tq_outer = min(s, DIFF_Q_STEP)
    n_qblocks = tq_outer // TQ
    n_chunks = s // TM
    vec = pl.BlockSpec((1, HEAD_DIM), lambda bi, h, qi: (0, 0))
    return pl.pallas_call(
        functools.partial(_diff_kernel, lambda_init=lambda_init, n_chunks=n_chunks, n_qblocks=n_qblocks,
                          unroll_c=min(n_chunks, DIFF_RUN_CHUNKS),
                          unroll_q=n_qblocks if n_chunks * n_qblocks <= DIFF_RUN_CHUNKS else 1),
        grid=(b, DA_HEADS, s // tq_outer),
        in_specs=[
            pl.BlockSpec((1, n_qblocks, 2 * HEAD_DIM, TQ), lambda bi, h, qi: (bi, qi, h, 0)),
            pl.BlockSpec((1, 1, s, 2 * HEAD_DIM), lambda bi, h, qi: (bi, h, 0, 0)),
            pl.BlockSpec((1, n_chunks, DA_VDIM, TM), lambda bi, h, qi: (bi, 0, h, 0)),
            vec, vec, vec, vec,
        ],
        out_specs=pl.BlockSpec((1, 1, tq_outer, DA_VDIM), lambda bi, h, qi: (bi, h, qi, 0)),
        out_shape=jax.ShapeDtypeStruct((b, DA_HEADS, s, DA_VDIM), BF16),
        scratch_shapes=[pltpu.VMEM((SUBLANES, LANES), F32)],
        compiler_params=_cparams(("arbitrary", "arbitrary", "arbitrary")),
        name="diff_attention",
    )(qa, ka, va, lq1, lk1, lq2, lk2)


def _window_kernel(q_ref, kp_ref, kc_ref, kn_ref, vp_ref, vc_ref, vn_ref, sink_ref, o_ref, *, seq_len):
    c = pl.program_id(1)
    step = kc_ref.shape[1]
    keys = jnp.concatenate([kp_ref[0], kc_ref[0], kn_ref[0]], axis=0)
    vals = jnp.concatenate([vp_ref[0, 0]] + [vc_ref[0, t] for t in range(vc_ref.shape[1])]
                           + [vn_ref[0, 0]], axis=1)
    sink = sink_ref[...] * LOG2E
    n_win = 3 * WINDOW
    kidx = lax.broadcasted_iota(jnp.int32, (n_win, WINDOW), 0)
    qidx = lax.broadcasted_iota(jnp.int32, (n_win, WINDOW), 1)
    band = jnp.abs(kidx - WINDOW - qidx) <= WINDOW
    zero_q = jnp.zeros((HEAD_DIM, WINDOW), BF16)

    ksq = keys.astype(F32)
    ksq = ksq * ksq
    lane = lax.broadcasted_iota(jnp.int32, ksq.shape, 1)
    kmax = []
    for g in range(WG_KV):
        mine = (lane >= g * HEAD_DIM) & (lane < (g + 1) * HEAD_DIM)
        n2 = jnp.sum(jnp.where(mine, ksq, 0.0), axis=-1, keepdims=True)
        kmax.append(jnp.sqrt(jnp.max(n2, axis=0, keepdims=True)))

    def attend(bounded):
        def logits(qs, g):
            blocks = []
            for hh in range(WG_GROUP):
                hd = g * WG_GROUP + hh
                qh = q_ref[0, qs // 2, hd * HEAD_DIM:(hd + 1) * HEAD_DIM,
                           (qs % 2) * WINDOW:(qs % 2 + 1) * WINDOW]
                pad = [qh, zero_q] if g == 0 else [zero_q, qh]
                blocks.append(jnp.concatenate(pad, axis=0))
            qpad = jnp.concatenate(blocks, axis=1)
            kwin = keys[qs * WINDOW:qs * WINDOW + n_win]
            return qpad, jnp.dot(kwin, qpad, preferred_element_type=F32)

        tiles = [(qs, g) for qs in range(step // WINDOW) for g in range(WG_KV)]
        l_min = jnp.full((1, WG_GROUP * WINDOW), jnp.inf, F32)
        outs = []
        nxt = logits(*tiles[0])
        for t, (qs, g) in enumerate(tiles):
            qpad, s = nxt
            if t + 1 < len(tiles):
                nxt = logits(*tiles[t + 1])
            kpos = c * step + (qs - 1) * WINDOW + kidx
            valid1 = band & (kpos >= 0) & (kpos < seq_len)
            bias1 = jnp.where(valid1, 0.0, -jnp.inf).astype(F32)
            s = s + jnp.concatenate([bias1] * WG_GROUP, axis=1)
            vwin = vals[:, qs * WINDOW:qs * WINDOW + n_win]
            sk = sink[:, g * WG_GROUP * WINDOW:(g + 1) * WG_GROUP * WINDOW]
            if bounded:
                qf = qpad.astype(F32)
                top = jnp.sqrt(jnp.sum(qf * qf, axis=0, keepdims=True)) * kmax[g] * BOUND_SLACK
            else:
                top = jnp.max(s, axis=0, keepdims=True)
            shift = jnp.maximum(top, sk)
            p = jnp.exp2(s - shift)
            l = jnp.sum(p, axis=0, keepdims=True) + jnp.exp2(sk - shift)
            l_min = jnp.minimum(l_min, l)
            pv = jnp.dot(vwin, p.astype(BF16), preferred_element_type=F32)
            o = pv[g * HEAD_DIM:(g + 1) * HEAD_DIM] / l
            for hh in range(WG_GROUP):
                outs.append(o[:, hh * WINDOW:(hh + 1) * WINDOW])
            if g == WG_KV - 1:
                o_all = jnp.concatenate(outs, axis=0)
                o_ref[0, qs * WINDOW:(qs + 1) * WINDOW, :] = o_all.T.astype(BF16)
                outs = []
        return l_min

    bounded_ok = jnp.min(attend(True)) >= MIN_COLUMN_SUM

    @pl.when(jnp.logical_not(bounded_ok))
    def _():
        attend(False)


def _window_call(qw, kw, vw, sink_row):
    b, s, _ = kw.shape
    ns = s // TM
    nblk = s // WINDOW
    step = WIN_TILES * TM
    per = step // WINDOW
    last = TM // WINDOW - 1
    return pl.pallas_call(
        functools.partial(_window_kernel, seq_len=s),
        grid=(b, s // step),
        in_specs=[
            pl.BlockSpec((1, step // TQ, WG_Q, TQ), lambda bi, c: (bi, c, 0, 0)),
            pl.BlockSpec((1, WINDOW, WG_K), lambda bi, c: (bi, jnp.maximum(c * per - 1, 0), 0)),
            pl.BlockSpec((1, step, WG_K), lambda bi, c: (bi, c, 0)),
            pl.BlockSpec((1, WINDOW, WG_K), lambda bi, c: (bi, jnp.minimum(c * per + per, nblk - 1), 0)),
            pl.BlockSpec((1, 1, WG_V, WINDOW), lambda bi, c: (bi, jnp.maximum(c * WIN_TILES - 1, 0), 0, last)),
            pl.BlockSpec((1, WIN_TILES, WG_V, TM), lambda bi, c: (bi, c, 0, 0)),
            pl.BlockSpec((1, 1, WG_V, WINDOW),
                         lambda bi, c: (bi, jnp.minimum(c * WIN_TILES + WIN_TILES, ns - 1), 0, 0)),
            pl.BlockSpec((1, WG_HEADS * WINDOW), lambda bi, c: (0, 0)),
        ],
        out_specs=pl.BlockSpec((1, step, WG_Q), lambda bi, c: (bi, c, 0)),
        out_shape=jax.ShapeDtypeStruct((b, s, WG_Q), BF16),
        compiler_params=_cparams(("arbitrary", "arbitrary")),
        name="window_gqa",
    )(qw, kw, kw, kw, vw, vw, vw, sink_row)


def _out_ffn_kernel(x_ref, oa_ref, ow_ref, mod_ref, gpre_ref, gpost_ref, wa_ref, ww_ref,
                    wg_ref, wu_ref, wd_ref, o_ref, a_ref, h_ref):
    n_sub = x_ref.shape[1] // SUB_ROWS

    def head(r):
        rows = _sub_rows(r)
        oa = jnp.concatenate([oa_ref[0, hd, rows, :] for hd in range(DA_HEADS)], axis=1)
        y = (jnp.dot(oa, wa_ref[...], preferred_element_type=F32)
             + jnp.dot(ow_ref[0, rows], ww_ref[...], preferred_element_type=F32))
        x = x_ref[0, rows] + _post(y, mod_ref, gpost_ref, 1)
        o_ref[0, rows] = x
        h_ref[_slot(r)] = _pre(x, mod_ref, gpre_ref, 2).astype(BF16)

    def tail(r, y):
        rows = _sub_rows(r)
        o_ref[0, rows] = o_ref[0, rows] + _post(y, mod_ref, gpost_ref, 2, coef=0.5)

    head(0)
    _gate_up(h_ref, a_ref, wg_ref, wu_ref, _slot(0))
    for r in range(1, n_sub):
        head(r)
        y = _down(a_ref, wd_ref, _slot(r - 1))
        _gate_up(h_ref, a_ref, wg_ref, wu_ref, _slot(r))
        tail(r - 1, y)
    tail(n_sub - 1, _down(a_ref, wd_ref, _slot(n_sub - 1)))


def _out_ffn_call(x, oa, ow, mod, gpre, gpost, wa, ww, wg, wu, wd):
    b, s, _ = x.shape
    tiles = OUT_FFN_TILES
    return pl.pallas_call(
        _out_ffn_kernel,
        grid=(b, s // (tiles * TM)),
        in_specs=[_row_spec(D_MODEL, tiles),
                  pl.BlockSpec((1, DA_HEADS, tiles * TM, DA_VDIM), lambda bi, si: (bi, 0, si, 0)),
                  _row_spec(WG_Q, tiles)] + _norm_specs()
        + [_resident((DA_V, D_MODEL)), _resident((WG_Q, D_MODEL))] + _ffn_specs(),
        out_specs=_row_spec(D_MODEL, tiles),
        out_shape=jax.ShapeDtypeStruct(x.shape, F32),
        scratch_shapes=[pltpu.VMEM((TM, D_FF), BF16), pltpu.VMEM((TM, D_MODEL), BF16)],
        compiler_params=_cparams(("arbitrary", "arbitrary")),
        name="out_proj_ffn",
    )(x, oa, ow, mod, gpre, gpost, wa, ww, wg, wu, wd)


def _rope_tables(seq_len):
    pos = jnp.arange(seq_len, dtype=F32)
    inv_freq = 1.0 / (ROPE_THETA ** (jnp.arange(0, HEAD_DIM, 2, dtype=F32) / HEAD_DIM))
    ang = pos[:, None] * inv_freq[None, :]
    cos = jnp.cos(ang)
    sin = jnp.sin(ang)
    cos_h = jnp.concatenate([cos, cos], axis=-1)
    sin_h = jnp.concatenate([-sin, sin], axis=-1)
    cos_k = jnp.concatenate([cos_h, cos_h], axis=-1)
    sin_k = jnp.concatenate([sin_h, sin_h], axis=-1)
    return cos_h.T, sin_h.T, cos_k, sin_k


def kernel(x_prompt, x_sample, c_prompt, c_sample, w_mod, b_mod, norm_pre, norm_post, w_ff_gate, w_ff_up,
           w_ff_down, w_in, w_out, lambda_q1, lambda_k1, lambda_q2, lambda_k2, sink):
    n_prompt = x_prompt.shape[0]
    c_all = jnp.concatenate([c_prompt, c_sample], axis=0)
    n_seq = c_all.shape[0]
    c_all = jnp.pad(c_all, ((0, -n_seq % SUBLANES), (0, 0)))
    mod_all = _mod_call(c_all, w_mod, b_mod)[:, :n_seq]
    mod_all = mod_all.reshape(DEPTH, n_seq, 3 * N_SUB, D_MODEL)

    wg = w_ff_gate.astype(BF16)
    wu = w_ff_up.astype(BF16)
    wd = w_ff_down.astype(BF16)
    o_qa, o_ka, o_va, o_qw, o_kw, o_vw = 0, DA_Q, DA_Q + DA_K, DA_Q + DA_K + DA_V, \
        DA_Q + DA_K + DA_V + WG_Q, DA_Q + DA_K + DA_V + WG_Q + WG_K
    w_in_b = w_in.astype(BF16)
    wt = jnp.concatenate([w_in_b[:, :, o_qa:o_ka], w_in_b[:, :, o_va:o_qw],
                          w_in_b[:, :, o_qw:o_kw], w_in_b[:, :, o_vw:]], axis=-1).transpose(0, 2, 1)
    wk = jnp.concatenate([w_in_b[:, :, o_ka:o_va], w_in_b[:, :, o_kw:o_vw]], axis=-1)
    w_out_b = w_out.astype(BF16)
    sink_rows = jnp.repeat(sink, WINDOW, axis=-1)

    groups = [(x_prompt, slice(0, n_prompt)), (x_sample, slice(n_prompt, None))]
    tables = {x.shape[1]: _rope_tables(x.shape[1]) for x, _ in groups}
    outs = []
    for x, rows in groups:
        tab = tables[x.shape[1]]
        for l in range(DEPTH):
            mod = mod_all[l, rows]
            lambda_init = 0.8 - 0.6 * math.exp(-0.3 * l)
            x, qa, va, qw, vw, ka, kw = _ffn_proj_call(x, mod, norm_pre[l], norm_post[l], wg[l, 0], wu[l, 0],
                                                       wd[l, 0], wt[l], wk[l], *tab)
            oa = _diff_call(qa, ka, va, lambda_q1[l:l + 1], lambda_k1[l:l + 1], lambda_q2[l:l + 1],
                            lambda_k2[l:l + 1], lambda_init)
            ow = _window_call(qw, kw, vw, sink_rows[l:l + 1])
            x = _out_ffn_call(x, oa, ow, mod, norm_pre[l], norm_post[l], w_out_b[l, :DA_V], w_out_b[l, DA_V:],
                              wg[l, 1], wu[l, 1], wd[l, 1])
        outs.append(x)
    return tuple(outs)
```

```python
import functools
import math

import jax
import jax.numpy as jnp
from jax import lax
from jax.experimental import pallas as pl
from jax.experimental.pallas import tpu as pltpu

F32 = jnp.float32
BF16 = jnp.bfloat16

D_MODEL = 1024
DEPTH = 4
HEAD_DIM = 64
HALF = HEAD_DIM // 2
WINDOW = 128
ROPE_THETA = 10000.0
DA_HEADS = 4
DA_VDIM = 2 * HEAD_DIM
WG_HEADS = 8
WG_KV = 2
WG_GROUP = WG_HEADS // WG_KV
DA_Q = DA_HEADS * 2 * HEAD_DIM
DA_K = DA_Q
DA_V = DA_HEADS * DA_VDIM
WG_Q = WG_HEADS * HEAD_DIM
WG_K = WG_KV * HEAD_DIM
WG_V = WG_KV * HEAD_DIM
MIX_WIDTH = DA_V + WG_Q
D_FF = 2816
N_SUB = 3
NORM_EPS = 1e-6
SUBLN_EPS = 1e-5
LOG2E = math.log2(math.e)
Q_SCALE = HEAD_DIM ** -0.5 * LOG2E
BOUND_SLACK = 1.0 + 2.0 ** -12
MIN_COLUMN_SUM = 2.0 ** -80

LANES = 128
SUBLANES = 8
TM = 512
TQ = 256
FF_CHUNK = 256
MOD_TN = 2304
VMEM_LIMIT = 56 * 1024 * 1024


def _cparams(sem):
    return pltpu.CompilerParams(dimension_semantics=sem, vmem_limit_bytes=VMEM_LIMIT)


def _resident(shape):
    return pl.BlockSpec(shape, lambda *_: (0, 0), pipeline_mode=pl.Buffered(1))


def _mod_kernel(c_ref, w_ref, b_ref, o_ref):
    c = c_ref[...]
    a = c * jax.nn.sigmoid(c)
    o_ref[0] = jnp.dot(a, w_ref[0], preferred_element_type=F32,
                       precision=lax.Precision.HIGHEST) + b_ref[0]


def _mod_call(c_all, w_mod, b_mod):
    nb = c_all.shape[0]
    n_out = w_mod.shape[-1]
    return pl.pallas_call(
        _mod_kernel,
        grid=(DEPTH, n_out // MOD_TN),
        in_specs=[
            pl.BlockSpec((nb, D_MODEL), lambda l, n: (0, 0)),
            pl.BlockSpec((1, D_MODEL, MOD_TN), lambda l, n: (l, 0, n)),
            pl.BlockSpec((1, 1, MOD_TN), lambda l, n: (l, 0, n)),
        ],
        out_specs=pl.BlockSpec((1, nb, MOD_TN), lambda l, n: (l, 0, n)),
        out_shape=jax.ShapeDtypeStruct((DEPTH, nb, n_out), F32),
        compiler_params=_cparams(("arbitrary", "arbitrary")),
        name="modulation",
    )(c_all, w_mod, b_mod.reshape(DEPTH, 1, n_out))


def _rms(x, eps):
    return x * lax.rsqrt(jnp.mean(x * x, axis=-1, keepdims=True) + eps)


def _pre(x, mod_ref, gpre_ref, j):
    shift = mod_ref[0, 3 * j:3 * j + 1, :]
    scale = mod_ref[0, 3 * j + 1:3 * j + 2, :]
    return _rms(x, NORM_EPS) * (gpre_ref[j:j + 1, :] * (1.0 + scale)) + shift


def _post(y, mod_ref, gpost_ref, j, coef=1.0):
    gate = mod_ref[0, 3 * j + 2:3 * j + 3, :]
    return _rms(y, NORM_EPS) * (coef * gate * gpost_ref[j:j + 1, :])


SUB_ROWS = TM // 2
HALF_ROWS = [pl.ds(r * SUB_ROWS, SUB_ROWS) for r in range(2)]
OUT_FFN_TILES = 2
FFN_PROJ_TILES = 2
WIN_TILES = 2


def _sub_rows(r):
    return pl.ds(r * SUB_ROWS, SUB_ROWS)


def _slot(r):
    return HALF_ROWS[r % 2]


def _gate_up(h_ref, a_ref, wg_ref, wu_ref, rows):
    for c in range(D_FF // FF_CHUNK):
        sl = slice(c * FF_CHUNK, (c + 1) * FF_CHUNK)
        g = jnp.dot(h_ref[rows], wg_ref[:, sl], preferred_element_type=F32)
        u = jnp.dot(h_ref[rows], wu_ref[:, sl], preferred_element_type=F32)
        a_ref[rows, sl] = (g * jax.nn.sigmoid(g) * u).astype(BF16)


def _down(a_ref, wd_ref, rows):
    return jnp.dot(a_ref[rows], wd_ref[...], preferred_element_type=F32)


def _row_spec(width, tiles=1):
    return pl.BlockSpec((1, tiles * TM, width), lambda bi, si: (bi, si, 0))


def _norm_specs():
    const = lambda bi, si: (0, 0)
    return [pl.BlockSpec((1, 3 * N_SUB, D_MODEL), lambda bi, si: (bi, 0, 0)),
            pl.BlockSpec((N_SUB, D_MODEL), const), pl.BlockSpec((N_SUB, D_MODEL), const)]


def _ffn_specs():
    return [_resident((D_MODEL, D_FF)), _resident((D_MODEL, D_FF)), _resident((D_FF, D_MODEL))]


def _rope_rows(y, cos_t, sin_t, n_groups):
    out = []
    for g in range(n_groups):
        blk = y[g * HEAD_DIM:(g + 1) * HEAD_DIM]
        swapped = jnp.concatenate([blk[HALF:], blk[:HALF]], axis=0)
        out.append((blk * cos_t + swapped * sin_t) * Q_SCALE)
    return jnp.concatenate(out, axis=0)


def _ffn_proj_kernel(x_ref, mod_ref, gpre_ref, gpost_ref, wg_ref, wu_ref, wd_ref, wt_ref, wk_ref,
                     cos_t_ref, sin_t_ref, cos_k_ref, sin_k_ref,
                     xo_ref, qa_ref, va_ref, qw_ref, vw_ref, ka_ref, kw_ref, a_ref, h_ref, hmix_ref):
    hm = SUB_ROWS
    n_sub = x_ref.shape[1] // SUB_ROWS
    lane = lax.broadcasted_iota(jnp.int32, (hm, LANES), 1)
    first_half = (lane % HEAD_DIM) < HALF

    def head(r):
        h_ref[_slot(r)] = _pre(x_ref[0, _sub_rows(r)], mod_ref, gpre_ref, 0).astype(BF16)

    def mid(r, y):
        rows = _sub_rows(r)
        x = x_ref[0, rows] + _post(y, mod_ref, gpost_ref, 0, coef=0.5)
        xo_ref[0, rows] = x
        hmix_ref[_slot(r)] = _pre(x, mod_ref, gpre_ref, 1).astype(BF16)

    def project(r):
        rows = _sub_rows(r)
        tile, half = divmod(r, TM // SUB_ROWS)
        cols = slice(half * hm, (half + 1) * hm)
        h = hmix_ref[_slot(r)]

        k = jnp.dot(h, wk_ref[...], preferred_element_type=F32)
        cos_k = cos_k_ref[rows, :]
        sin_k = sin_k_ref[rows, :]
        for c in range((DA_K + WG_K) // LANES):
            blk = k[:, c * LANES:(c + 1) * LANES]
            swapped = jnp.where(first_half, pltpu.roll(blk, LANES - HALF, 1), pltpu.roll(blk, HALF, 1))
            kr = (blk * cos_k + swapped * sin_k).astype(BF16)
            if c < DA_K // LANES:
                ka_ref[0, rows, c * LANES:(c + 1) * LANES] = kr
            else:
                kw_ref[0, rows, :] = kr

        def t_rows(lo, hi):
            return lax.dot_general(wt_ref[lo:hi, :], h, (((1,), (1,)), ((), ())),
                                   preferred_element_type=F32)

        tok = slice(r * hm, (r + 1) * hm)
        cos_t = cos_t_ref[:, tok]
        sin_t = sin_t_ref[:, tok]
        qa_ref[0, r] = _rope_rows(t_rows(0, DA_Q), cos_t, sin_t, DA_Q // HEAD_DIM).astype(BF16)
        qw_ref[0, r] = _rope_rows(t_rows(DA_Q + DA_V, DA_Q + DA_V + WG_Q), cos_t, sin_t,
                                  WG_Q // HEAD_DIM).astype(BF16)
        va_ref[0, tile, :, cols] = t_rows(DA_Q, DA_Q + DA_V).astype(BF16)
        vw_ref[0, tile, :, cols] = t_rows(DA_Q + DA_V + WG_Q, DA_Q + DA_V + WG_Q + WG_V).astype(BF16)

    head(0)
    _gate_up(h_ref, a_ref, wg_ref, wu_ref, _slot(0))
    for r in range(1, n_sub):
        head(r)
        y = _down(a_ref, wd_ref, _slot(r - 1))
        if r >= 2:
            project(r - 2)
        _gate_up(h_ref, a_ref, wg_ref, wu_ref, _slot(r))
        mid(r - 1, y)
    y = _down(a_ref, wd_ref, _slot(n_sub - 1))
    project(n_sub - 2)
    mid(n_sub - 1, y)
    project(n_sub - 1)


def _ffn_proj_call(x, mod, gpre, gpost, wg, wu, wd, wt, wk, cos_t, sin_t, cos_k, sin_k):
    b, s, _ = x.shape
    ns = s // TM
    tiles = FFN_PROJ_TILES
    return pl.pallas_call(
        _ffn_proj_kernel,
        grid=(b, ns // tiles),
        in_specs=[_row_spec(D_MODEL, tiles)] + _norm_specs() + _ffn_specs() + [
            _resident(wt.shape),
            _resident(wk.shape),
            pl.BlockSpec((HEAD_DIM, tiles * TM), lambda bi, si: (0, si)),
            pl.BlockSpec((HEAD_DIM, tiles * TM), lambda bi, si: (0, si)),
            pl.BlockSpec((tiles * TM, LANES), lambda bi, si: (si, 0)),
            pl.BlockSpec((tiles * TM, LANES), lambda bi, si: (si, 0)),
        ],
        out_specs=[
            _row_spec(D_MODEL, tiles),
            pl.BlockSpec((1, tiles * TM // TQ, DA_Q, TQ), lambda bi, si: (bi, si, 0, 0)),
            pl.BlockSpec((1, tiles, DA_V, TM), lambda bi, si: (bi, si, 0, 0)),
            pl.BlockSpec((1, tiles * TM // TQ, WG_Q, TQ), lambda bi, si: (bi, si, 0, 0)),
            pl.BlockSpec((1, tiles, WG_V, TM), lambda bi, si: (bi, si, 0, 0)),
            _row_spec(DA_K, tiles),
            _row_spec(WG_K, tiles),
        ],
        out_shape=[
            jax.ShapeDtypeStruct(x.shape, F32),
            jax.ShapeDtypeStruct((b, s // TQ, DA_Q, TQ), BF16),
            jax.ShapeDtypeStruct((b, ns, DA_V, TM), BF16),
            jax.ShapeDtypeStruct((b, s // TQ, WG_Q, TQ), BF16),
            jax.ShapeDtypeStruct((b, ns, WG_V, TM), BF16),
            jax.ShapeDtypeStruct((b, s, DA_K), BF16),
            jax.ShapeDtypeStruct((b, s, WG_K), BF16),
        ],
        scratch_shapes=[pltpu.VMEM((TM, D_FF), BF16), pltpu.VMEM((TM, D_MODEL), BF16),
                        pltpu.VMEM((TM, D_MODEL), BF16)],
        compiler_params=_cparams(("arbitrary", "arbitrary")),
        name="ffn_in_proj",
    )(x, mod, gpre, gpost, wg, wu, wd, wt, wk, cos_t, sin_t, cos_k, sin_k)


def _diff_kernel(q_ref, k_ref, v_ref, lq1_ref, lk1_ref, lq2_ref, lk2_ref, o_ref, knorm_ref, *,
                 lambda_init, n_chunks, n_qblocks, unroll_c, unroll_q):
    lam = (jnp.exp(jnp.sum(lq1_ref[...] * lk1_ref[...], axis=-1, keepdims=True))
           - jnp.exp(jnp.sum(lq2_ref[...] * lk2_ref[...], axis=-1, keepdims=True)) + lambda_init)
    row = lax.broadcasted_iota(jnp.int32, (2 * HEAD_DIM, TQ), 0)

    @pl.when(pl.program_id(2) == 0)
    def _():
        def body(j, mx):
            kc = k_ref[0, pl.ds(pl.multiple_of(j * TM, TM), TM), :].astype(F32)
            return jnp.maximum(mx, jnp.sum(kc * kc, axis=-1, keepdims=True))
        mx = lax.fori_loop(0, n_chunks, body, jnp.zeros((TM, 1), F32))
        knorm_ref[...] = jnp.broadcast_to(jnp.sqrt(jnp.max(mx, axis=0, keepdims=True)), knorm_ref.shape)

    def padded_q(qb):
        q = q_ref[0, qb]
        zero = jnp.zeros_like(q)
        return jnp.concatenate([jnp.where(row < HEAD_DIM, q, zero),
                                jnp.where(row >= HEAD_DIM, q, zero)], axis=1)

    def p_times_v(j, pb):
        vt = v_ref[0, j]
        return jnp.concatenate(
            [jnp.dot(vt, pb[:, :TQ], preferred_element_type=F32),
             jnp.dot(vt, pb[:, TQ:], preferred_element_type=F32)], axis=1)

    def finish(qb, l, acc):
        o = acc / l
        o = o[:, :TQ] - lam * o[:, TQ:]
        o = o * lax.rsqrt(jnp.mean(o * o, axis=0, keepdims=True) + SUBLN_EPS) * (1.0 - lambda_init)
        o_ref[0, pl.ds(pl.multiple_of(qb * TQ, TQ), TQ), :] = o.T.astype(BF16)

    def key_chunk(j):
        return k_ref[0, pl.ds(pl.multiple_of(j * TM, TM), TM), :]

    kmax = knorm_ref[0:1, :]
    kmax = jnp.concatenate([kmax] * (2 * TQ // LANES), axis=1)

    def q_block_bounded(qb, l_min):
        qpad = padded_q(qb)
        qf = qpad.astype(F32)
        shift = jnp.sqrt(jnp.sum(qf * qf, axis=0, keepdims=True)) * kmax * BOUND_SLACK

        comps = [slice(0, TQ), slice(TQ, 2 * TQ)]

        def logits(j):
            k = key_chunk(j)
            return [jnp.dot(k, qpad[:, c], preferred_element_type=F32) for c in comps]

        def run(j0, st):
            l8, acc = [list(t) for t in st]
            s = logits(j0)
            for i in range(unroll_c):
                s_next = logits(j0 + i + 1) if i + 1 < unroll_c else None
                vt = v_ref[0, j0 + i]
                for ci, c in enumerate(comps):
                    p = jnp.exp2(s[ci] - shift[:, c])
                    l8[ci] = l8[ci] + jnp.sum(p.reshape(TM // SUBLANES, SUBLANES, TQ), axis=0)
                    acc[ci] = acc[ci] + jnp.dot(vt, p.astype(BF16), preferred_element_type=F32)
                s = s_next
            return tuple(l8), tuple(acc)

        st = ((jnp.zeros((SUBLANES, TQ), F32),) * 2, (jnp.zeros((DA_VDIM, TQ), F32),) * 2)
        if n_chunks == unroll_c:
            st = run(0, st)
        else:
            st = lax.fori_loop(0, n_chunks // unroll_c, lambda g, st: run(g * unroll_c, st), st)
        l8 = jnp.concatenate(st[0], axis=1)
        acc = jnp.concatenate(st[1], axis=1)
        l = jnp.sum(l8, axis=0, keepdims=True)
        finish(qb, l, acc)
        return jnp.minimum(l_min, l)

    def q_block_online(qb, carry):
        qpad = padded_q(qb)

        def chunk(j, st):
            m, l, acc = st
            s = jnp.dot(key_chunk(j), qpad, preferred_element_type=F32)
            m_new = jnp.maximum(m, jnp.max(s, axis=0, keepdims=True))
            alpha = jnp.exp2(m - m_new)
            p = jnp.exp2(s - m_new)
            l = alpha * l + jnp.sum(p, axis=0, keepdims=True)
            return m_new, l, acc * alpha + p_times_v(j, p.astype(BF16))

        init = (jnp.full((1, 2 * TQ), -jnp.inf, F32), jnp.zeros((1, 2 * TQ), F32),
                jnp.zeros((DA_VDIM, 2 * TQ), F32))
        _, l, acc = lax.fori_loop(0, n_chunks, chunk, init)
        finish(qb, l, acc)
        return carry

    l_min = lax.fori_loop(0, n_qblocks, q_block_bounded, jnp.full((1, 2 * TQ), jnp.inf, F32),
                          unroll=unroll_q)
    bounded_ok = jnp.min(l_min) >= MIN_COLUMN_SUM

    @pl.when(jnp.logical_not(bounded_ok))
    def _():
        lax.fori_loop(0, n_qblocks, q_block_online, 0)


def _diff_call(qa, ka, va, lq1, lk1, lq2, lk2, lambda_init):
    b, s, _ = ka.shape
    tq_outer = min(s, 2048)
    n_qblocks = tq_outer // TQ
    n_chunks = s // TM
    vec = pl.BlockSpec((1, HEAD_DIM), lambda bi, h, qi: (0, 0))
    return pl.pallas_call(
        functools.partial(_diff_kernel, lambda_init=lambda_init, n_chunks=n_chunks, n_qblocks=n_qblocks,
                          unroll_c=min(n_chunks, 32), unroll_q=n_qblocks if n_chunks <= 4 else 1),
        grid=(b, DA_HEADS, s // tq_outer),
        in_specs=[
            pl.BlockSpec((1, n_qblocks, 2 * HEAD_DIM, TQ), lambda bi, h, qi: (bi, qi, h, 0)),
            pl.BlockSpec((1, s, 2 * HEAD_DIM), lambda bi, h, qi: (bi, 0, h)),
            pl.BlockSpec((1, n_chunks, DA_VDIM, TM), lambda bi, h, qi: (bi, 0, h, 0)),
            vec, vec, vec, vec,
        ],
        out_specs=pl.BlockSpec((1, tq_outer, DA_VDIM), lambda bi, h, qi: (bi, qi, h)),
        out_shape=jax.ShapeDtypeStruct((b, s, DA_V), BF16),
        scratch_shapes=[pltpu.VMEM((SUBLANES, LANES), F32)],
        compiler_params=_cparams(("arbitrary", "arbitrary", "arbitrary")),
        name="diff_attention",
    )(qa, ka, va, lq1, lk1, lq2, lk2)


def _window_kernel(q_ref, kp_ref, kc_ref, kn_ref, vp_ref, vc_ref, vn_ref, sink_ref, o_ref, *, seq_len):
    c = pl.program_id(1)
    step = kc_ref.shape[1]
    keys = jnp.concatenate([kp_ref[0], kc_ref[0], kn_ref[0]], axis=0)
    vals = jnp.concatenate([vp_ref[0, 0]] + [vc_ref[0, t] for t in range(vc_ref.shape[1])]
                           + [vn_ref[0, 0]], axis=1)
    sink = sink_ref[...] * LOG2E
    n_win = 3 * WINDOW
    kidx = lax.broadcasted_iota(jnp.int32, (n_win, WINDOW), 0)
    qidx = lax.broadcasted_iota(jnp.int32, (n_win, WINDOW), 1)
    band = jnp.abs(kidx - WINDOW - qidx) <= WINDOW
    zero_q = jnp.zeros((HEAD_DIM, WINDOW), BF16)

    ksq = keys.astype(F32)
    ksq = ksq * ksq
    lane = lax.broadcasted_iota(jnp.int32, ksq.shape, 1)
    kmax = []
    for g in range(WG_KV):
        mine = (lane >= g * HEAD_DIM) & (lane < (g + 1) * HEAD_DIM)
        n2 = jnp.sum(jnp.where(mine, ksq, 0.0), axis=-1, keepdims=True)
        kmax.append(jnp.sqrt(jnp.max(n2, axis=0, keepdims=True)))

    def attend(bounded):
        def logits(qs, g):
            blocks = []
            for hh in range(WG_GROUP):
                hd = g * WG_GROUP + hh
                qh = q_ref[0, qs // 2, hd * HEAD_DIM:(hd + 1) * HEAD_DIM,
                           (qs % 2) * WINDOW:(qs % 2 + 1) * WINDOW]
                pad = [qh, zero_q] if g == 0 else [zero_q, qh]
                blocks.append(jnp.concatenate(pad, axis=0))
            qpad = jnp.concatenate(blocks, axis=1)
            kwin = keys[qs * WINDOW:qs * WINDOW + n_win]
            return qpad, jnp.dot(kwin, qpad, preferred_element_type=F32)

        tiles = [(qs, g) for qs in range(step // WINDOW) for g in range(WG_KV)]
        l_min = jnp.full((1, WG_GROUP * WINDOW), jnp.inf, F32)
        outs = []
        nxt = logits(*tiles[0])
        for t, (qs, g) in enumerate(tiles):
            qpad, s = nxt
            if t + 1 < len(tiles):
                nxt = logits(*tiles[t + 1])
            kpos = c * step + (qs - 1) * WINDOW + kidx
            valid1 = band & (kpos >= 0) & (kpos < seq_len)
            bias1 = jnp.where(valid1, 0.0, -jnp.inf).astype(F32)
            s = s + jnp.concatenate([bias1] * WG_GROUP, axis=1)
            vwin = vals[:, qs * WINDOW:qs * WINDOW + n_win]
            sk = sink[:, g * WG_GROUP * WINDOW:(g + 1) * WG_GROUP * WINDOW]
            if bounded:
                qf = qpad.astype(F32)
                top = jnp.sqrt(jnp.sum(qf * qf, axis=0, keepdims=True)) * kmax[g] * BOUND_SLACK
            else:
                top = jnp.max(s, axis=0, keepdims=True)
            shift = jnp.maximum(top, sk)
            p = jnp.exp2(s - shift)
            l = jnp.sum(p, axis=0, keepdims=True) + jnp.exp2(sk - shift)
            l_min = jnp.minimum(l_min, l)
            pv = jnp.dot(vwin, p.astype(BF16), preferred_element_type=F32)
            o = pv[g * HEAD_DIM:(g + 1) * HEAD_DIM] / l
            for hh in range(WG_GROUP):
                outs.append(o[:, hh * WINDOW:(hh + 1) * WINDOW])
            if g == WG_KV - 1:
                o_all = jnp.concatenate(outs, axis=0)
                o_ref[0, qs * WINDOW:(qs + 1) * WINDOW, :] = o_all.T.astype(BF16)
                outs = []
        return l_min

    bounded_ok = jnp.min(attend(True)) >= MIN_COLUMN_SUM

    @pl.when(jnp.logical_not(bounded_ok))
    def _():
        attend(False)


def _window_call(qw, kw, vw, sink_row):
    b, s, _ = kw.shape
    ns = s // TM
    nblk = s // WINDOW
    step = WIN_TILES * TM
    per = step // WINDOW
    last = TM // WINDOW - 1
    return pl.pallas_call(
        functools.partial(_window_kernel, seq_len=s),
        grid=(b, s // step),
        in_specs=[
            pl.BlockSpec((1, step // TQ, WG_Q, TQ), lambda bi, c: (bi, c, 0, 0)),
            pl.BlockSpec((1, WINDOW, WG_K), lambda bi, c: (bi, jnp.maximum(c * per - 1, 0), 0)),
            pl.BlockSpec((1, step, WG_K), lambda bi, c: (bi, c, 0)),
            pl.BlockSpec((1, WINDOW, WG_K), lambda bi, c: (bi, jnp.minimum(c * per + per, nblk - 1), 0)),
            pl.BlockSpec((1, 1, WG_V, WINDOW), lambda bi, c: (bi, jnp.maximum(c * WIN_TILES - 1, 0), 0, last)),
            pl.BlockSpec((1, WIN_TILES, WG_V, TM), lambda bi, c: (bi, c, 0, 0)),
            pl.BlockSpec((1, 1, WG_V, WINDOW),
                         lambda bi, c: (bi, jnp.minimum(c * WIN_TILES + WIN_TILES, ns - 1), 0, 0)),
            pl.BlockSpec((1, WG_HEADS * WINDOW), lambda bi, c: (0, 0)),
        ],
        out_specs=pl.BlockSpec((1, step, WG_Q), lambda bi, c: (bi, c, 0)),
        out_shape=jax.ShapeDtypeStruct((b, s, WG_Q), BF16),
        compiler_params=_cparams(("arbitrary", "arbitrary")),
        name="window_gqa",
    )(qw, kw, kw, kw, vw, vw, vw, sink_row)


def _out_ffn_kernel(x_ref, oa_ref, ow_ref, mod_ref, gpre_ref, gpost_ref, wa_ref, ww_ref,
                    wg_ref, wu_ref, wd_ref, o_ref, a_ref, h_ref):
    n_sub = x_ref.shape[1] // SUB_ROWS

    def head(r):
        rows = _sub_rows(r)
        y = (jnp.dot(oa_ref[0, rows], wa_ref[...], preferred_element_type=F32)
             + jnp.dot(ow_ref[0, rows], ww_ref[...], preferred_element_type=F32))
        x = x_ref[0, rows] + _post(y, mod_ref, gpost_ref, 1)
        o_ref[0, rows] = x
        h_ref[_slot(r)] = _pre(x, mod_ref, gpre_ref, 2).astype(BF16)

    def tail(r, y):
        rows = _sub_rows(r)
        o_ref[0, rows] = o_ref[0, rows] + _post(y, mod_ref, gpost_ref, 2, coef=0.5)

    head(0)
    _gate_up(h_ref, a_ref, wg_ref, wu_ref, _slot(0))
    for r in range(1, n_sub):
        head(r)
        y = _down(a_ref, wd_ref, _slot(r - 1))
        _gate_up(h_ref, a_ref, wg_ref, wu_ref, _slot(r))
        tail(r - 1, y)
    tail(n_sub - 1, _down(a_ref, wd_ref, _slot(n_sub - 1)))


def _out_ffn_call(x, oa, ow, mod, gpre, gpost, wa, ww, wg, wu, wd):
    b, s, _ = x.shape
    tiles = OUT_FFN_TILES
    return pl.pallas_call(
        _out_ffn_kernel,
        grid=(b, s // (tiles * TM)),
        in_specs=[_row_spec(D_MODEL, tiles), _row_spec(DA_V, tiles), _row_spec(WG_Q, tiles)] + _norm_specs()
        + [_resident((DA_V, D_MODEL)), _resident((WG_Q, D_MODEL))] + _ffn_specs(),
        out_specs=_row_spec(D_MODEL, tiles),
        out_shape=jax.ShapeDtypeStruct(x.shape, F32),
        scratch_shapes=[pltpu.VMEM((TM, D_FF), BF16), pltpu.VMEM((TM, D_MODEL), BF16)],
        compiler_params=_cparams(("arbitrary", "arbitrary")),
        name="out_proj_ffn",
    )(x, oa, ow, mod, gpre, gpost, wa, ww, wg, wu, wd)


def _rope_tables(seq_len):
    pos = jnp.arange(seq_len, dtype=F32)
    inv_freq = 1.0 / (ROPE_THETA ** (jnp.arange(0, HEAD_DIM, 2, dtype=F32) / HEAD_DIM))
    ang = pos[:, None] * inv_freq[None, :]
    cos = jnp.cos(ang)
    sin = jnp.sin(ang)
    cos_h = jnp.concatenate([cos, cos], axis=-1)
    sin_h = jnp.concatenate([-sin, sin], axis=-1)
    cos_k = jnp.concatenate([cos_h, cos_h], axis=-1)
    sin_k = jnp.concatenate([sin_h, sin_h], axis=-1)
    return cos_h.T, sin_h.T, cos_k, sin_k


def kernel(x_prompt, x_sample, c_prompt, c_sample, w_mod, b_mod, norm_pre, norm_post, w_ff_gate, w_ff_up,
           w_ff_down, w_in, w_out, lambda_q1, lambda_k1, lambda_q2, lambda_k2, sink):
    n_prompt = x_prompt.shape[0]
    c_all = jnp.concatenate([c_prompt, c_sample], axis=0)
    n_seq = c_all.shape[0]
    c_all = jnp.pad(c_all, ((0, -n_seq % SUBLANES), (0, 0)))
    mod_all = _mod_call(c_all, w_mod, b_mod)[:, :n_seq]
    mod_all = mod_all.reshape(DEPTH, n_seq, 3 * N_SUB, D_MODEL)

    wg = w_ff_gate.astype(BF16)
    wu = w_ff_up.astype(BF16)
    wd = w_ff_down.astype(BF16)
    o_qa, o_ka, o_va, o_qw, o_kw, o_vw = 0, DA_Q, DA_Q + DA_K, DA_Q + DA_K + DA_V, \
        DA_Q + DA_K + DA_V + WG_Q, DA_Q + DA_K + DA_V + WG_Q + WG_K
    w_in_b = w_in.astype(BF16)
    wt = jnp.concatenate([w_in_b[:, :, o_qa:o_ka], w_in_b[:, :, o_va:o_qw],
                          w_in_b[:, :, o_qw:o_kw], w_in_b[:, :, o_vw:]], axis=-1).transpose(0, 2, 1)
    wk = jnp.concatenate([w_in_b[:, :, o_ka:o_va], w_in_b[:, :, o_kw:o_vw]], axis=-1)
    w_out_b = w_out.astype(BF16)
    sink_rows = jnp.repeat(sink, WINDOW, axis=-1)

    groups = [(x_prompt, slice(0, n_prompt)), (x_sample, slice(n_prompt, None))]
    tables = {x.shape[1]: _rope_tables(x.shape[1]) for x, _ in groups}
    outs = []
    for x, rows in groups:
        tab = tables[x.shape[1]]
        for l in range(DEPTH):
            mod = mod_all[l, rows]
            lambda_init = 0.8 - 0.6 * math.exp(-0.3 * l)
            x, qa, va, qw, vw, ka, kw = _ffn_proj_call(x, mod, norm_pre[l], norm_post[l], wg[l, 0], wu[l, 0],
                                                       wd[l, 0], wt[l], wk[l], *tab)
            oa = _diff_call(qa, ka, va, lambda_q1[l:l + 1], lambda_k1[l:l + 1], lambda_q2[l:l + 1],
                            lambda_k2[l:l + 1], lambda_init)
            ow = _window_call(qw, kw, vw, sink_rows[l:l + 1])
            x = _out_ffn_call(x, oa, ow, mod, norm_pre[l], norm_post[l], w_out_b[l, :DA_V], w_out_b[l, DA_V:],
                              wg[l, 1], wu[l, 1], wd[l, 1])
        outs.append(x)
    return tuple(outs)
```

```python
import functools
import math

import jax
import jax.numpy as jnp
from jax import lax
from jax.experimental import pallas as pl
from jax.experimental.pallas import tpu as pltpu

F32 = jnp.float32
BF16 = jnp.bfloat16

D_MODEL = 1024
DEPTH = 4
HEAD_DIM = 64
HALF = HEAD_DIM // 2
WINDOW = 128
ROPE_THETA = 10000.0
DA_HEADS = 4
DA_VDIM = 2 * HEAD_DIM
WG_HEADS = 8
WG_KV = 2
WG_GROUP = WG_HEADS // WG_KV
DA_Q = DA_HEADS * 2 * HEAD_DIM
DA_K = DA_Q
DA_V = DA_HEADS * DA_VDIM
WG_Q = WG_HEADS * HEAD_DIM
WG_K = WG_KV * HEAD_DIM
WG_V = WG_KV * HEAD_DIM
MIX_WIDTH = DA_V + WG_Q
D_FF = 2816
N_SUB = 3
NORM_EPS = 1e-6
SUBLN_EPS = 1e-5
LOG2E = math.log2(math.e)
Q_SCALE = HEAD_DIM ** -0.5 * LOG2E
BOUND_SLACK = 1.0 + 2.0 ** -12
MIN_COLUMN_SUM = 2.0 ** -80

LANES = 128
SUBLANES = 8
TM = 512
TQ = 256
FF_CHUNK = 256
MOD_TN = 2304
VMEM_LIMIT = 56 * 1024 * 1024


def _cparams(sem):
    return pltpu.CompilerParams(dimension_semantics=sem, vmem_limit_bytes=VMEM_LIMIT)


def _resident(shape):
    return pl.BlockSpec(shape, lambda *_: (0, 0), pipeline_mode=pl.Buffered(1))


def _mod_kernel(c_ref, w_ref, b_ref, o_ref):
    c = c_ref[...]
    a = c * jax.nn.sigmoid(c)
    o_ref[0] = jnp.dot(a, w_ref[0], preferred_element_type=F32,
                       precision=lax.Precision.HIGHEST) + b_ref[0]


def _mod_call(c_all, w_mod, b_mod):
    nb = c_all.shape[0]
    n_out = w_mod.shape[-1]
    return pl.pallas_call(
        _mod_kernel,
        grid=(DEPTH, n_out // MOD_TN),
        in_specs=[
            pl.BlockSpec((nb, D_MODEL), lambda l, n: (0, 0)),
            pl.BlockSpec((1, D_MODEL, MOD_TN), lambda l, n: (l, 0, n)),
            pl.BlockSpec((1, 1, MOD_TN), lambda l, n: (l, 0, n)),
        ],
        out_specs=pl.BlockSpec((1, nb, MOD_TN), lambda l, n: (l, 0, n)),
        out_shape=jax.ShapeDtypeStruct((DEPTH, nb, n_out), F32),
        compiler_params=_cparams(("arbitrary", "arbitrary")),
        name="modulation",
    )(c_all, w_mod, b_mod.reshape(DEPTH, 1, n_out))


def _rms(x, eps):
    return x * lax.rsqrt(jnp.mean(x * x, axis=-1, keepdims=True) + eps)


def _pre(x, mod_ref, gpre_ref, j):
    shift = mod_ref[0, 3 * j:3 * j + 1, :]
    scale = mod_ref[0, 3 * j + 1:3 * j + 2, :]
    return _rms(x, NORM_EPS) * (gpre_ref[j:j + 1, :] * (1.0 + scale)) + shift


def _post(y, mod_ref, gpost_ref, j, coef=1.0):
    gate = mod_ref[0, 3 * j + 2:3 * j + 3, :]
    return _rms(y, NORM_EPS) * (coef * gate * gpost_ref[j:j + 1, :])


SUB_ROWS = TM // 2
HALF_ROWS = [pl.ds(r * SUB_ROWS, SUB_ROWS) for r in range(2)]
OUT_FFN_TILES = 2
FFN_PROJ_TILES = 2
WIN_TILES = 2


def _sub_rows(r):
    return pl.ds(r * SUB_ROWS, SUB_ROWS)


def _slot(r):
    return HALF_ROWS[r % 2]


def _gate_up(h_ref, a_ref, wg_ref, wu_ref, rows):
    for c in range(D_FF // FF_CHUNK):
        sl = slice(c * FF_CHUNK, (c + 1) * FF_CHUNK)
        g = jnp.dot(h_ref[rows], wg_ref[:, sl], preferred_element_type=F32)
        u = jnp.dot(h_ref[rows], wu_ref[:, sl], preferred_element_type=F32)
        a_ref[rows, sl] = (g * jax.nn.sigmoid(g) * u).astype(BF16)


def _down(a_ref, wd_ref, rows):
    return jnp.dot(a_ref[rows], wd_ref[...], preferred_element_type=F32)


def _row_spec(width, tiles=1):
    return pl.BlockSpec((1, tiles * TM, width), lambda bi, si: (bi, si, 0))


def _norm_specs():
    const = lambda bi, si: (0, 0)
    return [pl.BlockSpec((1, 3 * N_SUB, D_MODEL), lambda bi, si: (bi, 0, 0)),
            pl.BlockSpec((N_SUB, D_MODEL), const), pl.BlockSpec((N_SUB, D_MODEL), const)]


def _ffn_specs():
    return [_resident((D_MODEL, D_FF)), _resident((D_MODEL, D_FF)), _resident((D_FF, D_MODEL))]


def _rope_rows(y, cos_t, sin_t, n_groups):
    out = []
    for g in range(n_groups):
        blk = y[g * HEAD_DIM:(g + 1) * HEAD_DIM]
        swapped = jnp.concatenate([blk[HALF:], blk[:HALF]], axis=0)
        out.append((blk * cos_t + swapped * sin_t) * Q_SCALE)
    return jnp.concatenate(out, axis=0)


def _ffn_proj_kernel(x_ref, mod_ref, gpre_ref, gpost_ref, wg_ref, wu_ref, wd_ref, wt_ref, wk_ref,
                     cos_t_ref, sin_t_ref, cos_k_ref, sin_k_ref,
                     xo_ref, qa_ref, va_ref, qw_ref, vw_ref, ka_ref, kw_ref, a_ref, h_ref, hmix_ref):
    hm = SUB_ROWS
    n_sub = x_ref.shape[1] // SUB_ROWS
    lane = lax.broadcasted_iota(jnp.int32, (hm, LANES), 1)
    first_half = (lane % HEAD_DIM) < HALF

    def head(r):
        h_ref[_slot(r)] = _pre(x_ref[0, _sub_rows(r)], mod_ref, gpre_ref, 0).astype(BF16)

    def mid(r, y):
        rows = _sub_rows(r)
        x = x_ref[0, rows] + _post(y, mod_ref, gpost_ref, 0, coef=0.5)
        xo_ref[0, rows] = x
        hmix_ref[_slot(r)] = _pre(x, mod_ref, gpre_ref, 1).astype(BF16)

    def project(r):
        rows = _sub_rows(r)
        tile, half = divmod(r, TM // SUB_ROWS)
        cols = slice(half * hm, (half + 1) * hm)
        h = hmix_ref[_slot(r)]

        k = jnp.dot(h, wk_ref[...], preferred_element_type=F32)
        cos_k = cos_k_ref[rows, :]
        sin_k = sin_k_ref[rows, :]
        for c in range((DA_K + WG_K) // LANES):
            blk = k[:, c * LANES:(c + 1) * LANES]
            swapped = jnp.where(first_half, pltpu.roll(blk, LANES - HALF, 1), pltpu.roll(blk, HALF, 1))
            kr = (blk * cos_k + swapped * sin_k).astype(BF16)
            if c < DA_K // LANES:
                ka_ref[0, rows, c * LANES:(c + 1) * LANES] = kr
            else:
                kw_ref[0, rows, :] = kr

        def t_rows(lo, hi):
            return lax.dot_general(wt_ref[lo:hi, :], h, (((1,), (1,)), ((), ())),
                                   preferred_element_type=F32)

        tok = slice(r * hm, (r + 1) * hm)
        cos_t = cos_t_ref[:, tok]
        sin_t = sin_t_ref[:, tok]
        qa_ref[0, r] = _rope_rows(t_rows(0, DA_Q), cos_t, sin_t, DA_Q // HEAD_DIM).astype(BF16)
        qw_ref[0, r] = _rope_rows(t_rows(DA_Q + DA_V, DA_Q + DA_V + WG_Q), cos_t, sin_t,
                                  WG_Q // HEAD_DIM).astype(BF16)
        va_ref[0, tile, :, cols] = t_rows(DA_Q, DA_Q + DA_V).astype(BF16)
        vw_ref[0, tile, :, cols] = t_rows(DA_Q + DA_V + WG_Q, DA_Q + DA_V + WG_Q + WG_V).astype(BF16)

    head(0)
    _gate_up(h_ref, a_ref, wg_ref, wu_ref, _slot(0))
    for r in range(1, n_sub):
        head(r)
        y = _down(a_ref, wd_ref, _slot(r - 1))
        if r >= 2:
            project(r - 2)
        _gate_up(h_ref, a_ref, wg_ref, wu_ref, _slot(r))
        mid(r - 1, y)
    y = _down(a_ref, wd_ref, _slot(n_sub - 1))
    project(n_sub - 2)
    mid(n_sub - 1, y)
    project(n_sub - 1)


def _ffn_proj_call(x, mod, gpre, gpost, wg, wu, wd, wt, wk, cos_t, sin_t, cos_k, sin_k):
    b, s, _ = x.shape
    ns = s // TM
    tiles = FFN_PROJ_TILES
    return pl.pallas_call(
        _ffn_proj_kernel,
        grid=(b, ns // tiles),
        in_specs=[_row_spec(D_MODEL, tiles)] + _norm_specs() + _ffn_specs() + [
            _resident(wt.shape),
            _resident(wk.shape),
            pl.BlockSpec((HEAD_DIM, tiles * TM), lambda bi, si: (0, si)),
            pl.BlockSpec((HEAD_DIM, tiles * TM), lambda bi, si: (0, si)),
            pl.BlockSpec((tiles * TM, LANES), lambda bi, si: (si, 0)),
            pl.BlockSpec((tiles * TM, LANES), lambda bi, si: (si, 0)),
        ],
        out_specs=[
            _row_spec(D_MODEL, tiles),
            pl.BlockSpec((1, tiles * TM // TQ, DA_Q, TQ), lambda bi, si: (bi, si, 0, 0)),
            pl.BlockSpec((1, tiles, DA_V, TM), lambda bi, si: (bi, si, 0, 0)),
            pl.BlockSpec((1, tiles * TM // TQ, WG_Q, TQ), lambda bi, si: (bi, si, 0, 0)),
            pl.BlockSpec((1, tiles, WG_V, TM), lambda bi, si: (bi, si, 0, 0)),
            _row_spec(DA_K, tiles),
            _row_spec(WG_K, tiles),
        ],
        out_shape=[
            jax.ShapeDtypeStruct(x.shape, F32),
            jax.ShapeDtypeStruct((b, s // TQ, DA_Q, TQ), BF16),
            jax.ShapeDtypeStruct((b, ns, DA_V, TM), BF16),
            jax.ShapeDtypeStruct((b, s // TQ, WG_Q, TQ), BF16),
            jax.ShapeDtypeStruct((b, ns, WG_V, TM), BF16),
            jax.ShapeDtypeStruct((b, s, DA_K), BF16),
            jax.ShapeDtypeStruct((b, s, WG_K), BF16),
        ],
        scratch_shapes=[pltpu.VMEM((TM, D_FF), BF16), pltpu.VMEM((TM, D_MODEL), BF16),
                        pltpu.VMEM((TM, D_MODEL), BF16)],
        compiler_params=_cparams(("arbitrary", "arbitrary")),
        name="ffn_in_proj",
    )(x, mod, gpre, gpost, wg, wu, wd, wt, wk, cos_t, sin_t, cos_k, sin_k)


def _diff_kernel(q_ref, k_ref, v_ref, lq1_ref, lk1_ref, lq2_ref, lk2_ref, o_ref, knorm_ref, *,
                 lambda_init, n_chunks, n_qblocks, unroll_c, unroll_q):
    lam = (jnp.exp(jnp.sum(lq1_ref[...] * lk1_ref[...], axis=-1, keepdims=True))
           - jnp.exp(jnp.sum(lq2_ref[...] * lk2_ref[...], axis=-1, keepdims=True)) + lambda_init)
    row = lax.broadcasted_iota(jnp.int32, (2 * HEAD_DIM, TQ), 0)

    @pl.when(pl.program_id(2) == 0)
    def _():
        def body(j, mx):
            kc = k_ref[0, pl.ds(pl.multiple_of(j * TM, TM), TM), :].astype(F32)
            return jnp.maximum(mx, jnp.sum(kc * kc, axis=-1, keepdims=True))
        mx = lax.fori_loop(0, n_chunks, body, jnp.zeros((TM, 1), F32))
        knorm_ref[...] = jnp.broadcast_to(jnp.sqrt(jnp.max(mx, axis=0, keepdims=True)), knorm_ref.shape)

    def padded_q(qb):
        q = q_ref[0, qb]
        zero = jnp.zeros_like(q)
        return jnp.concatenate([jnp.where(row < HEAD_DIM, q, zero),
                                jnp.where(row >= HEAD_DIM, q, zero)], axis=1)

    def p_times_v(j, pb):
        vt = v_ref[0, j]
        return jnp.concatenate(
            [jnp.dot(vt, pb[:, :TQ], preferred_element_type=F32),
             jnp.dot(vt, pb[:, TQ:], preferred_element_type=F32)], axis=1)

    def finish(qb, l, acc):
        o = acc / l
        o = o[:, :TQ] - lam * o[:, TQ:]
        o = o * lax.rsqrt(jnp.mean(o * o, axis=0, keepdims=True) + SUBLN_EPS) * (1.0 - lambda_init)
        o_ref[0, pl.ds(pl.multiple_of(qb * TQ, TQ), TQ), :] = o.T.astype(BF16)

    def key_chunk(j):
        return k_ref[0, pl.ds(pl.multiple_of(j * TM, TM), TM), :]

    kmax = knorm_ref[0:1, :]
    kmax = jnp.concatenate([kmax] * (2 * TQ // LANES), axis=1)

    def q_block_bounded(qb, l_min):
        qpad = padded_q(qb)
        qf = qpad.astype(F32)
        shift = jnp.sqrt(jnp.sum(qf * qf, axis=0, keepdims=True)) * kmax * BOUND_SLACK

        def logits(j):
            return jnp.dot(key_chunk(j), qpad, preferred_element_type=F32)

        def run(j0, st):
            l8, acc = st
            s = logits(j0)
            for i in range(unroll_c):
                s_next = logits(j0 + i + 1) if i + 1 < unroll_c else None
                p = jnp.exp2(s - shift)
                l8 = l8 + jnp.sum(p.reshape(TM // SUBLANES, SUBLANES, 2 * TQ), axis=0)
                acc = acc + p_times_v(j0 + i, p.astype(BF16))
                s = s_next
            return l8, acc

        st = (jnp.zeros((SUBLANES, 2 * TQ), F32), jnp.zeros((DA_VDIM, 2 * TQ), F32))
        if n_chunks == unroll_c:
            st = run(0, st)
        else:
            st = lax.fori_loop(0, n_chunks // unroll_c, lambda g, st: run(g * unroll_c, st), st)
        l8, acc = st
        l = jnp.sum(l8, axis=0, keepdims=True)
        finish(qb, l, acc)
        return jnp.minimum(l_min, l)

    def q_block_online(qb, carry):
        qpad = padded_q(qb)

        def chunk(j, st):
            m, l, acc = st
            s = jnp.dot(key_chunk(j), qpad, preferred_element_type=F32)
            m_new = jnp.maximum(m, jnp.max(s, axis=0, keepdims=True))
            alpha = jnp.exp2(m - m_new)
            p = jnp.exp2(s - m_new)
            l = alpha * l + jnp.sum(p, axis=0, keepdims=True)
            return m_new, l, acc * alpha + p_times_v(j, p.astype(BF16))

        init = (jnp.full((1, 2 * TQ), -jnp.inf, F32), jnp.zeros((1, 2 * TQ), F32),
                jnp.zeros((DA_VDIM, 2 * TQ), F32))
        _, l, acc = lax.fori_loop(0, n_chunks, chunk, init)
        finish(qb, l, acc)
        return carry

    l_min = lax.fori_loop(0, n_qblocks, q_block_bounded, jnp.full((1, 2 * TQ), jnp.inf, F32),
                          unroll=unroll_q)
    bounded_ok = jnp.min(l_min) >= MIN_COLUMN_SUM

    @pl.when(jnp.logical_not(bounded_ok))
    def _():
        lax.fori_loop(0, n_qblocks, q_block_online, 0)


def _diff_call(qa, ka, va, lq1, lk1, lq2, lk2, lambda_init):
    b, s, _ = ka.shape
    tq_outer = min(s, 2048)
    n_qblocks = tq_outer // TQ
    n_chunks = s // TM
    vec = pl.BlockSpec((1, HEAD_DIM), lambda bi, h, qi: (0, 0))
    return pl.pallas_call(
        functools.partial(_diff_kernel, lambda_init=lambda_init, n_chunks=n_chunks, n_qblocks=n_qblocks,
                          unroll_c=min(n_chunks, 32), unroll_q=n_qblocks if n_chunks <= 4 else 1),
        grid=(b, DA_HEADS, s // tq_outer),
        in_specs=[
            pl.BlockSpec((1, n_qblocks, 2 * HEAD_DIM, TQ), lambda bi, h, qi: (bi, qi, h, 0)),
            pl.BlockSpec((1, s, 2 * HEAD_DIM), lambda bi, h, qi: (bi, 0, h)),
            pl.BlockSpec((1, n_chunks, DA_VDIM, TM), lambda bi, h, qi: (bi, 0, h, 0)),
            vec, vec, vec, vec,
        ],
        out_specs=pl.BlockSpec((1, tq_outer, DA_VDIM), lambda bi, h, qi: (bi, qi, h)),
        out_shape=jax.ShapeDtypeStruct((b, s, DA_V), BF16),
        scratch_shapes=[pltpu.VMEM((SUBLANES, LANES), F32)],
        compiler_params=_cparams(("arbitrary", "arbitrary", "arbitrary")),
        name="diff_attention",
    )(qa, ka, va, lq1, lk1, lq2, lk2)


def _window_kernel(q_ref, kp_ref, kc_ref, kn_ref, vp_ref, vc_ref, vn_ref, sink_ref, o_ref, *, seq_len):
    c = pl.program_id(1)
    step = kc_ref.shape[1]
    keys = jnp.concatenate([kp_ref[0], kc_ref[0], kn_ref[0]], axis=0)
    vals = jnp.concatenate([vp_ref[0, 0]] + [vc_ref[0, t] for t in range(vc_ref.shape[1])]
                           + [vn_ref[0, 0]], axis=1)
    sink = sink_ref[...] * LOG2E
    n_win = 3 * WINDOW
    kidx = lax.broadcasted_iota(jnp.int32, (n_win, WINDOW), 0)
    qidx = lax.broadcasted_iota(jnp.int32, (n_win, WINDOW), 1)
    band = jnp.abs(kidx - WINDOW - qidx) <= WINDOW
    zero_q = jnp.zeros((HEAD_DIM, WINDOW), BF16)

    ksq = keys.astype(F32)
    ksq = ksq * ksq
    lane = lax.broadcasted_iota(jnp.int32, ksq.shape, 1)
    kmax = []
    for g in range(WG_KV):
        mine = (lane >= g * HEAD_DIM) & (lane < (g + 1) * HEAD_DIM)
        n2 = jnp.sum(jnp.where(mine, ksq, 0.0), axis=-1, keepdims=True)
        kmax.append(jnp.sqrt(jnp.max(n2, axis=0, keepdims=True)))

    def attend(bounded):
        gw = WG_GROUP * WINDOW
        kmax_row = jnp.concatenate([jnp.broadcast_to(kmax[g], (1, gw)) for g in range(WG_KV)], axis=1)

        def logits(qs):
            blocks = []
            for hd in range(WG_HEADS):
                qh = q_ref[0, qs // 2, hd * HEAD_DIM:(hd + 1) * HEAD_DIM,
                           (qs % 2) * WINDOW:(qs % 2 + 1) * WINDOW]
                pad = [qh, zero_q] if hd < WG_GROUP else [zero_q, qh]
                blocks.append(jnp.concatenate(pad, axis=0))
            qpad = jnp.concatenate(blocks, axis=1)
            kwin = keys[qs * WINDOW:qs * WINDOW + n_win]
            return qpad, jnp.dot(kwin, qpad, preferred_element_type=F32)

        n_tiles = step // WINDOW
        l_min = jnp.full((1, WG_HEADS * WINDOW), jnp.inf, F32)
        nxt = logits(0)
        for qs in range(n_tiles):
            qpad, s = nxt
            if qs + 1 < n_tiles:
                nxt = logits(qs + 1)
            kpos = c * step + (qs - 1) * WINDOW + kidx
            valid1 = band & (kpos >= 0) & (kpos < seq_len)
            bias1 = jnp.where(valid1, 0.0, -jnp.inf).astype(F32)
            s = s + jnp.concatenate([bias1] * WG_HEADS, axis=1)
            vwin = vals[:, qs * WINDOW:qs * WINDOW + n_win]
            if bounded:
                qf = qpad.astype(F32)
                top = jnp.sqrt(jnp.sum(qf * qf, axis=0, keepdims=True)) * kmax_row * BOUND_SLACK
            else:
                top = jnp.max(s, axis=0, keepdims=True)
            shift = jnp.maximum(top, sink)
            p = jnp.exp2(s - shift)
            l = jnp.sum(p, axis=0, keepdims=True) + jnp.exp2(sink - shift)
            l_min = jnp.minimum(l_min, l)
            pv = jnp.dot(vwin, p.astype(BF16), preferred_element_type=F32)
            outs = []
            for g in range(WG_KV):
                o = pv[g * HEAD_DIM:(g + 1) * HEAD_DIM, g * gw:(g + 1) * gw] / l[:, g * gw:(g + 1) * gw]
                for hh in range(WG_GROUP):
                    outs.append(o[:, hh * WINDOW:(hh + 1) * WINDOW])
            o_all = jnp.concatenate(outs, axis=0)
            o_ref[0, qs * WINDOW:(qs + 1) * WINDOW, :] = o_all.T.astype(BF16)
        return l_min

    bounded_ok = jnp.min(attend(True)) >= MIN_COLUMN_SUM

    @pl.when(jnp.logical_not(bounded_ok))
    def _():
        attend(False)


def _window_call(qw, kw, vw, sink_row):
    b, s, _ = kw.shape
    ns = s // TM
    nblk = s // WINDOW
    step = WIN_TILES * TM
    per = step // WINDOW
    last = TM // WINDOW - 1
    return pl.pallas_call(
        functools.partial(_window_kernel, seq_len=s),
        grid=(b, s // step),
        in_specs=[
            pl.BlockSpec((1, step // TQ, WG_Q, TQ), lambda bi, c: (bi, c, 0, 0)),
            pl.BlockSpec((1, WINDOW, WG_K), lambda bi, c: (bi, jnp.maximum(c * per - 1, 0), 0)),
            pl.BlockSpec((1, step, WG_K), lambda bi, c: (bi, c, 0)),
            pl.BlockSpec((1, WINDOW, WG_K), lambda bi, c: (bi, jnp.minimum(c * per + per, nblk - 1), 0)),
            pl.BlockSpec((1, 1, WG_V, WINDOW), lambda bi, c: (bi, jnp.maximum(c * WIN_TILES - 1, 0), 0, last)),
            pl.BlockSpec((1, WIN_TILES, WG_V, TM), lambda bi, c: (bi, c, 0, 0)),
            pl.BlockSpec((1, 1, WG_V, WINDOW),
                         lambda bi, c: (bi, jnp.minimum(c * WIN_TILES + WIN_TILES, ns - 1), 0, 0)),
            pl.BlockSpec((1, WG_HEADS * WINDOW), lambda bi, c: (0, 0)),
        ],
        out_specs=pl.BlockSpec((1, step, WG_Q), lambda bi, c: (bi, c, 0)),
        out_shape=jax.ShapeDtypeStruct((b, s, WG_Q), BF16),
        compiler_params=_cparams(("arbitrary", "arbitrary")),
        name="window_gqa",
    )(qw, kw, kw, kw, vw, vw, vw, sink_row)


def _out_ffn_kernel(x_ref, oa_ref, ow_ref, mod_ref, gpre_ref, gpost_ref, wa_ref, ww_ref,
                    wg_ref, wu_ref, wd_ref, o_ref, a_ref, h_ref):
    n_sub = x_ref.shape[1] // SUB_ROWS

    def head(r):
        rows = _sub_rows(r)
        y = (jnp.dot(oa_ref[0, rows], wa_ref[...], preferred_element_type=F32)
             + jnp.dot(ow_ref[0, rows], ww_ref[...], preferred_element_type=F32))
        x = x_ref[0, rows] + _post(y, mod_ref, gpost_ref, 1)
        o_ref[0, rows] = x
        h_ref[_slot(r)] = _pre(x, mod_ref, gpre_ref, 2).astype(BF16)

    def tail(r, y):
        rows = _sub_rows(r)
        o_ref[0, rows] = o_ref[0, rows] + _post(y, mod_ref, gpost_ref, 2, coef=0.5)

    head(0)
    _gate_up(h_ref, a_ref, wg_ref, wu_ref, _slot(0))
    for r in range(1, n_sub):
        head(r)
        y = _down(a_ref, wd_ref, _slot(r - 1))
        _gate_up(h_ref, a_ref, wg_ref, wu_ref, _slot(r))
        tail(r - 1, y)
    tail(n_sub - 1, _down(a_ref, wd_ref, _slot(n_sub - 1)))


def _out_ffn_call(x, oa, ow, mod, gpre, gpost, wa, ww, wg, wu, wd):
    b, s, _ = x.shape
    tiles = OUT_FFN_TILES
    return pl.pallas_call(
        _out_ffn_kernel,
        grid=(b, s // (tiles * TM)),
        in_specs=[_row_spec(D_MODEL, tiles), _row_spec(DA_V, tiles), _row_spec(WG_Q, tiles)] + _norm_specs()
        + [_resident((DA_V, D_MODEL)), _resident((WG_Q, D_MODEL))] + _ffn_specs(),
        out_specs=_row_spec(D_MODEL, tiles),
        out_shape=jax.ShapeDtypeStruct(x.shape, F32),
        scratch_shapes=[pltpu.VMEM((TM, D_FF), BF16), pltpu.VMEM((TM, D_MODEL), BF16)],
        compiler_params=_cparams(("arbitrary", "arbitrary")),
        name="out_proj_ffn",
    )(x, oa, ow, mod, gpre, gpost, wa, ww, wg, wu, wd)


def _rope_tables(seq_len):
    pos = jnp.arange(seq_len, dtype=F32)
    inv_freq = 1.0 / (ROPE_THETA ** (jnp.arange(0, HEAD_DIM, 2, dtype=F32) / HEAD_DIM))
    ang = pos[:, None] * inv_freq[None, :]
    cos = jnp.cos(ang)
    sin = jnp.sin(ang)
    cos_h = jnp.concatenate([cos, cos], axis=-1)
    sin_h = jnp.concatenate([-sin, sin], axis=-1)
    cos_k = jnp.concatenate([cos_h, cos_h], axis=-1)
    sin_k = jnp.concatenate([sin_h, sin_h], axis=-1)
    return cos_h.T, sin_h.T, cos_k, sin_k


def kernel(x_prompt, x_sample, c_prompt, c_sample, w_mod, b_mod, norm_pre, norm_post, w_ff_gate, w_ff_up,
           w_ff_down, w_in, w_out, lambda_q1, lambda_k1, lambda_q2, lambda_k2, sink):
    n_prompt = x_prompt.shape[0]
    c_all = jnp.concatenate([c_prompt, c_sample], axis=0)
    n_seq = c_all.shape[0]
    c_all = jnp.pad(c_all, ((0, -n_seq % SUBLANES), (0, 0)))
    mod_all = _mod_call(c_all, w_mod, b_mod)[:, :n_seq]
    mod_all = mod_all.reshape(DEPTH, n_seq, 3 * N_SUB, D_MODEL)

    wg = w_ff_gate.astype(BF16)
    wu = w_ff_up.astype(BF16)
    wd = w_ff_down.astype(BF16)
    o_qa, o_ka, o_va, o_qw, o_kw, o_vw = 0, DA_Q, DA_Q + DA_K, DA_Q + DA_K + DA_V, \
        DA_Q + DA_K + DA_V + WG_Q, DA_Q + DA_K + DA_V + WG_Q + WG_K
    w_in_b = w_in.astype(BF16)
    wt = jnp.concatenate([w_in_b[:, :, o_qa:o_ka], w_in_b[:, :, o_va:o_qw],
                          w_in_b[:, :, o_qw:o_kw], w_in_b[:, :, o_vw:]], axis=-1).transpose(0, 2, 1)
    wk = jnp.concatenate([w_in_b[:, :, o_ka:o_va], w_in_b[:, :, o_kw:o_vw]], axis=-1)
    w_out_b = w_out.astype(BF16)
    sink_rows = jnp.repeat(sink, WINDOW, axis=-1)

    groups = [(x_prompt, slice(0, n_prompt)), (x_sample, slice(n_prompt, None))]
    tables = {x.shape[1]: _rope_tables(x.shape[1]) for x, _ in groups}
    outs = []
    for x, rows in groups:
        tab = tables[x.shape[1]]
        for l in range(DEPTH):
            mod = mod_all[l, rows]
            lambda_init = 0.8 - 0.6 * math.exp(-0.3 * l)
            x, qa, va, qw, vw, ka, kw = _ffn_proj_call(x, mod, norm_pre[l], norm_post[l], wg[l, 0], wu[l, 0],
                                                       wd[l, 0], wt[l], wk[l], *tab)
            oa = _diff_call(qa, ka, va, lambda_q1[l:l + 1], lambda_k1[l:l + 1], lambda_q2[l:l + 1],
                            lambda_k2[l:l + 1], lambda_init)
            ow = _window_call(qw, kw, vw, sink_rows[l:l + 1])
            x = _out_ffn_call(x, oa, ow, mod, norm_pre[l], norm_post[l], w_out_b[l, :DA_V], w_out_b[l, DA_V:],
                              wg[l, 1], wu[l, 1], wd[l, 1])
        outs.append(x)
    return tuple(outs)
```
